```python
import jax, jax.numpy as jnp
from jax import lax
import numpy as np

D_MODEL = 4096
BATCH = 1
SEQ = 8192
DEPTH = 1

CHUNK = 64
Q_BLOCK = 128
D_MIX = D_MODEL
FOX_HEADS = 16
FOX_HEAD_DIM = 128
FOX_WIDTH = FOX_HEADS * FOX_HEAD_DIM
HGRN_HEADS = 16
HGRN_DK = 128
HGRN_DV = 128
HGRN_KWIDTH = HGRN_HEADS * HGRN_DK
HGRN_VWIDTH = HGRN_HEADS * HGRN_DV
D_FF = 11008
CONV_WIDTH = 3
N_MOD = 6
EPS = 1e-6

FOX_Q0 = 0
FOX_K0 = FOX_Q0 + FOX_WIDTH
FOX_V0 = FOX_K0 + FOX_WIDTH
FOX_F0 = FOX_V0 + FOX_WIDTH
HG_Q0 = FOX_F0 + FOX_HEADS
HG_F0 = HG_Q0 + HGRN_KWIDTH
HG_I0 = HG_F0 + HGRN_KWIDTH
HG_G0 = HG_I0 + HGRN_VWIDTH
IN_COLS = HG_G0 + HGRN_VWIDTH

kernel_name = "hybrid_fox_hgrn2_convffn_adaln"


def rms_norm(x, g):
    xf = x.astype(jnp.float32)
    y = xf * lax.rsqrt(jnp.mean(xf * xf, axis=-1, keepdims=True) + EPS)
    return (y * g.astype(jnp.float32)).astype(x.dtype)


def modulate(h, shift, scale):
    return h * (1 + scale[:, None, :]) + shift[:, None, :]


def forgetting_attention(q, k, v, log_f):
    B, T, H, Dh = q.shape
    nb = T // Q_BLOCK
    cum = jnp.cumsum(log_f, axis=1).transpose(0, 2, 1)
    qh = q.transpose(0, 2, 1, 3)
    kh = k.transpose(0, 2, 1, 3)
    vh = v.transpose(0, 2, 1, 3)
    qb = qh.reshape(B, H, nb, Q_BLOCK, Dh).transpose(2, 0, 1, 3, 4)
    cb = cum.reshape(B, H, nb, Q_BLOCK).transpose(2, 0, 1, 3)
    pos = jnp.arange(T)
    pb = pos.reshape(nb, Q_BLOCK)
    scale = Dh ** -0.5

    def block(args):
        q_blk, c_blk, p_blk = args
        s = jnp.einsum('bhqd,bhkd->bhqk', q_blk, kh).astype(jnp.float32) * scale
        s = s + c_blk[..., :, None] - cum[..., None, :]
        mask = pos[None, :] <= p_blk[:, None]
        s = jnp.where(mask, s, -jnp.inf)
        p = jax.nn.softmax(s, axis=-1)
        return jnp.einsum('bhqk,bhkd->bhqd', p.astype(vh.dtype), vh)

    ob = lax.map(block, (qb, cb, pb))
    return ob.transpose(1, 0, 3, 2, 4).reshape(B, T, H, Dh)


def hgrn2_recurrence(q, f, i, lb):
    B, T, H, Dk = q.shape
    Dv = i.shape[-1]
    n = T // CHUNK
    lbh = lb.reshape(H, Dk)
    fg = lbh + (1 - lbh) * jax.nn.sigmoid(f.astype(jnp.float32))
    log_f = jnp.log(fg)
    kk = 1 - fg
    qf = jax.nn.silu(q.astype(jnp.float32))

    def chunks(a):
        return a.reshape(B, n, CHUNK, H, a.shape[-1]).transpose(1, 0, 3, 2, 4)

    xs = (chunks(qf), chunks(kk), chunks(log_f), chunks(i.astype(jnp.float32)))
    causal = jnp.tril(jnp.ones((CHUNK, CHUNK), dtype=bool))

    def step(S, inp):
        qc, kc, lc, ic = inp
        b = jnp.cumsum(lc, axis=2)
        o_inter = jnp.einsum('bhtk,bhkv->bhtv', qc * jnp.exp(b), S)
        rel = jnp.where(causal[:, :, None], b[:, :, :, None, :] - b[:, :, None, :, :], -jnp.inf)
        A = jnp.einsum('bhtk,bhsk,bhtsk->bhts', qc, kc, jnp.exp(rel))
        o_intra = jnp.einsum('bhts,bhsv->bhtv', A, ic)
        b_last = b[:, :, -1:, :]
        S = jnp.exp(b_last[:, :, 0, :, None]) * S + jnp.einsum(
            'bhsk,bhsv->bhkv', kc * jnp.exp(b_last - b), ic)
        return S, o_inter + o_intra

    S0 = jnp.zeros((B, H, Dk, Dv), jnp.float32)
    _, o = lax.scan(step, S0, xs)
    return o.transpose(1, 0, 3, 2, 4).reshape(B, T, H, Dv)


def causal_depthwise_conv(u, w, b):
    T = u.shape[1]
    up = jnp.pad(u, ((0, 0), (CONV_WIDTH - 1, 0), (0, 0)))
    y = b + w[0] * up[:, 0:T]
    for j in range(1, CONV_WIDTH):
        y = y + w[j] * up[:, j:j + T]
    return y


def setup_inputs(seed: int = 0) -> dict:
    key = jax.random.key(seed)
    ks = jax.random.split(key, 17)
    f32 = jnp.float32
    nrm = lambda k, shape, s: jax.random.normal(k, shape, f32) * s
    return {
        "x": nrm(ks[0], (BATCH, SEQ, D_MODEL), 1.0),
        "c": nrm(ks[1], (BATCH, D_MODEL), 1.0),
        "w_ada": nrm(ks[2], (DEPTH, D_MODEL, N_MOD * D_MODEL), 0.5 * D_MODEL ** -0.5),
        "b_ada": nrm(ks[3], (DEPTH, N_MOD * D_MODEL), 0.02),
        "g_mix_norm": 1.0 + nrm(ks[4], (DEPTH, D_MODEL), 0.02),
        "w_in": nrm(ks[5], (DEPTH, D_MODEL, IN_COLS), D_MODEL ** -0.5),
        "b_fox_f": nrm(ks[6], (DEPTH, FOX_HEADS), 0.1),
        "hgrn_lb_logits": nrm(ks[7], (DEPTH + 1, HGRN_KWIDTH), 1.0),
        "g_fox_out": 1.0 + nrm(ks[8], (DEPTH, FOX_WIDTH), 0.02),
        "g_hgrn_out": 1.0 + nrm(ks[9], (DEPTH, HGRN_VWIDTH), 0.02),
        "w_out": nrm(ks[10], (DEPTH, D_MIX, D_MODEL), D_MIX ** -0.5),
        "g_ffn_norm": 1.0 + nrm(ks[11], (DEPTH, D_MODEL), 0.02),
        "w_up": nrm(ks[12], (DEPTH, D_MODEL, 2 * D_FF), D_MODEL ** -0.5),
        "conv_w": nrm(ks[13], (DEPTH, CONV_WIDTH, 2 * D_FF), CONV_WIDTH ** -0.5),
        "conv_b": nrm(ks[14], (DEPTH, 2 * D_FF), 0.02),
        "w_down": nrm(ks[15], (DEPTH, D_FF, D_MODEL), D_FF ** -0.5),
        "g_final": 1.0 + nrm(ks[16], (D_MODEL,), 0.02),
    }


def reference(x, c, w_ada, b_ada, g_mix_norm, w_in, b_fox_f, hgrn_lb_logits, g_fox_out,
              g_hgrn_out, w_out, g_ffn_norm, w_up, conv_w, conv_b, w_down, g_final):
    B, T, _ = x.shape
    lb_table = jnp.cumsum(jax.nn.softmax(hgrn_lb_logits.astype(jnp.float32), axis=0), axis=0)
    cond = jax.nn.silu(c)
    for l in range(DEPTH):
        mod = cond @ w_ada[l] + b_ada[l]
        sh1, sc1, gt1, sh2, sc2, gt2 = jnp.split(mod, N_MOD, axis=-1)

        h = modulate(rms_norm(x, g_mix_norm[l]), sh1, sc1)
        proj = h @ w_in[l]
        fq = proj[..., FOX_Q0:FOX_K0].reshape(B, T, FOX_HEADS, FOX_HEAD_DIM)
        fk = proj[..., FOX_K0:FOX_V0].reshape(B, T, FOX_HEADS, FOX_HEAD_DIM)
        fv = proj[..., FOX_V0:FOX_F0].reshape(B, T, FOX_HEADS, FOX_HEAD_DIM)
        log_f = jax.nn.log_sigmoid((proj[..., FOX_F0:HG_Q0] + b_fox_f[l]).astype(jnp.float32))
        hq = proj[..., HG_Q0:HG_F0].reshape(B, T, HGRN_HEADS, HGRN_DK)
        hf = proj[..., HG_F0:HG_I0].reshape(B, T, HGRN_HEADS, HGRN_DK)
        hi = proj[..., HG_I0:HG_G0].reshape(B, T, HGRN_HEADS, HGRN_DV)
        hg = proj[..., HG_G0:IN_COLS].reshape(B, T, HGRN_HEADS, HGRN_DV)

        o_fox = forgetting_attention(fq, fk, fv, log_f)
        o_fox = rms_norm(o_fox, g_fox_out[l].reshape(FOX_HEADS, FOX_HEAD_DIM))
        o_hg = hgrn2_recurrence(hq, hf, hi, lb_table[l]).astype(x.dtype)
        o_hg = rms_norm(o_hg, g_hgrn_out[l].reshape(HGRN_HEADS, HGRN_DV)) * jax.nn.silu(hg)

        mixed = jnp.concatenate([o_fox.reshape(B, T, FOX_WIDTH),
                                 o_hg.reshape(B, T, HGRN_VWIDTH)], axis=-1)
        x = x + gt1[:, None, :] * (mixed @ w_out[l])

        h = modulate(rms_norm(x, g_ffn_norm[l]), sh2, sc2)
        u = causal_depthwise_conv(h @ w_up[l], conv_w[l], conv_b[l])
        a, v = jnp.split(u, 2, axis=-1)
        x = x + gt2[:, None, :] * ((jax.nn.silu(a) * v) @ w_down[l])
    return rms_norm(x, g_final)
```

```python
import functools
import math

import numpy as np
import jax
import jax.numpy as jnp
from jax import lax
from jax.experimental import pallas as pl
from jax.experimental.pallas import tpu as pltpu

F32 = jnp.float32
BF16 = jnp.bfloat16

EPS = 1e-6
CONV_WIDTH = 3
N_MOD = 6
HEAD_DIM = 128
HGRN_CHUNK = 64
LANES = 128
SUBLANES = 8
VMEM_LIMIT = 56 * 1024 * 1024


def _params(*sem):
    return pltpu.CompilerParams(dimension_semantics=sem, vmem_limit_bytes=VMEM_LIMIT)


def _tile(n, pref):
    if n <= pref:
        return n
    t = pref
    while n % t:
        t //= 2
    return t


def _ada_kernel(cb_ref, w_ref, b_ref, o_ref, cond_ref):
    @pl.when(pl.program_id(0) == 0)
    def _():
        cb = cb_ref[...]
        cond_ref[...] = cb * jax.nn.sigmoid(cb)

    tn = o_ref.shape[1]
    for j in range(tn // LANES):
        sl = slice(j * LANES, (j + 1) * LANES)
        col = jnp.sum(w_ref[:, sl] * cond_ref[...], axis=0, keepdims=True)
        o_ref[:, sl] = col + b_ref[:, sl]


def _ada_mod(c, w_ada, b_ada):
    d, n = w_ada.shape
    cb = jnp.broadcast_to(c.reshape(d, 1), (d, LANES))
    tn = _tile(n, 512)
    return pl.pallas_call(
        _ada_kernel,
        out_shape=jax.ShapeDtypeStruct((1, n), F32),
        grid=(n // tn,),
        in_specs=[pl.BlockSpec((d, LANES), lambda j: (0, 0)),
                  pl.BlockSpec((d, tn), lambda j: (0, j)),
                  pl.BlockSpec((1, tn), lambda j: (0, j))],
        out_specs=pl.BlockSpec((1, tn), lambda j: (0, j)),
        scratch_shapes=[pltpu.VMEM((d, LANES), F32)],
        compiler_params=_params("arbitrary"),
        name="ada_mod",
    )(cb, w_ada, b_ada.reshape(1, n))


def _norm_mod_kernel(x_ref, g_ref, sc_ref, sh_ref, o_ref):
    x = x_ref[...]
    y = x * lax.rsqrt(jnp.mean(x * x, axis=-1, keepdims=True) + EPS) * g_ref[...]
    o_ref[...] = (y * (1.0 + sc_ref[...]) + sh_ref[...]).astype(o_ref.dtype)


def _norm_kernel(x_ref, g_ref, o_ref):
    x = x_ref[...]
    y = x * lax.rsqrt(jnp.mean(x * x, axis=-1, keepdims=True) + EPS) * g_ref[...]
    o_ref[...] = y.astype(o_ref.dtype)


def _norm_mod(x, g, sc, sh, out_dtype):
    t, d = x.shape
    tm = _tile(t, 256)
    row = pl.BlockSpec((tm, d), lambda i: (i, 0))
    vec = pl.BlockSpec((1, d), lambda i: (0, 0))
    if sc is None:
        kern, args, specs = _norm_kernel, (x, g), [row, vec]
    else:
        kern, args, specs = _norm_mod_kernel, (x, g, sc, sh), [row, vec, vec, vec]
    return pl.pallas_call(
        kern,
        out_shape=jax.ShapeDtypeStruct((t, d), out_dtype),
        grid=(t // tm,),
        in_specs=specs,
        out_specs=row,
        compiler_params=_params("parallel"),
        name="rms_norm",
    )(*args)


def _proj_kernel(*refs, n_a, has_res):
    a_refs = refs[:n_a]
    w_ref = refs[n_a]
    pos = n_a + 1
    if has_res:
        x_ref, g_ref = refs[pos], refs[pos + 1]
        pos += 2
    o_ref, wb_ref = refs[pos], refs[pos + 1]

    @pl.when(pl.program_id(1) == 0)
    def _():
        wb_ref[...] = w_ref[...].astype(BF16)

    acc = None
    k0 = 0
    for a_ref in a_refs:
        ks = a_ref.shape[1]
        part = jnp.dot(a_ref[...], wb_ref[k0:k0 + ks, :], preferred_element_type=F32)
        acc = part if acc is None else acc + part
        k0 += ks
    if has_res:
        acc = x_ref[...] + g_ref[...] * acc
    o_ref[...] = acc.astype(o_ref.dtype)


def _proj(a_list, w, col0, n, out_dtype, res=None, gate=None, name="proj"):
    m = a_list[0].shape[0]
    k = w.shape[0]
    assert sum(a.shape[1] for a in a_list) == k
    tm = _tile(m, 1024)
    tn = _tile(n, 512)
    assert col0 % tn == 0
    cb = col0 // tn
    in_specs = [pl.BlockSpec((tm, a.shape[1]), lambda j, i: (i, 0)) for a in a_list]
    in_specs.append(pl.BlockSpec((k, tn), lambda j, i: (0, cb + j)))
    args = list(a_list) + [w]
    if res is not None:
        in_specs += [pl.BlockSpec((tm, tn), lambda j, i: (i, j)),
                     pl.BlockSpec((1, tn), lambda j, i: (0, j))]
        args += [res, gate]
    return pl.pallas_call(
        functools.partial(_proj_kernel, n_a=len(a_list), has_res=res is not None),
        out_shape=jax.ShapeDtypeStruct((m, n), out_dtype),
        grid=(n // tn, m // tm),
        in_specs=in_specs,
        out_specs=pl.BlockSpec((tm, tn), lambda j, i: (i, j)),
        scratch_shapes=[pltpu.VMEM((k, tn), BF16)],
        compiler_params=_params("parallel", "arbitrary"),
        name=name,
    )(*args)


def _split3(x):
    hi = x.astype(BF16)
    r1 = x - hi.astype(F32)
    mid = r1.astype(BF16)
    lo = (r1 - mid.astype(F32)).astype(BF16)
    return hi, mid, lo


def _cum_kernel(ff_ref, b_ref, o_ref, carry_ref):
    @pl.when(pl.program_id(0) == 0)
    def _():
        carry_ref[...] = jnp.zeros_like(carry_ref)

    z = ff_ref[...] + b_ref[...]
    logf = jnp.minimum(z, 0.0) - jnp.log1p(jnp.exp(-jnp.abs(z)))
    tt = z.shape[0]
    row = lax.broadcasted_iota(jnp.int32, (tt, tt), 0)
    col = lax.broadcasted_iota(jnp.int32, (tt, tt), 1)
    tril = jnp.where(col <= row, 1.0, 0.0).astype(BF16)
    local = None
    for piece in _split3(logf):
        part = jnp.dot(tril, piece, preferred_element_type=F32)
        local = part if local is None else local + part
    out = local + carry_ref[0:1, :]
    o_ref[...] = out
    carry_ref[...] = jnp.broadcast_to(out[tt - 1:tt, :], carry_ref.shape)


def _fox_cum(ff, b_pad):
    t, n = ff.shape
    tt = _tile(t, 256)
    return pl.pallas_call(
        _cum_kernel,
        out_shape=jax.ShapeDtypeStruct((t, n), F32),
        grid=(t // tt,),
        in_specs=[pl.BlockSpec((tt, n), lambda i: (i, 0)),
                  pl.BlockSpec((1, n), lambda i: (0, 0))],
        out_specs=pl.BlockSpec((tt, n), lambda i: (i, 0)),
        scratch_shapes=[pltpu.VMEM((SUBLANES, n), F32)],
        compiler_params=_params("arbitrary"),
        name="fox_cum",
    )(ff, b_pad)


def _fox_kernel(qi_ref, kj_ref, q_ref, k_ref, v_ref, ccol_ref, crow_ref, g_ref, o_ref,
                m_ref, l_ref, acc_ref, ct_ref, *, scale):
    h = pl.program_id(0)
    i = qi_ref[pl.program_id(1)]
    j = kj_ref[pl.program_id(1)]
    tq = q_ref.shape[0]
    tk = k_ref.shape[0]

    @pl.when(j == 0)
    def _():
        m_ref[...] = jnp.full_like(m_ref, -jnp.inf)
        l_ref[...] = jnp.zeros_like(l_ref)
        acc_ref[...] = jnp.zeros_like(acc_ref)
        lane = lax.broadcasted_iota(jnp.int32, ccol_ref.shape, 1)
        ct_ref[...] = jnp.sum(jnp.where(lane == h, ccol_ref[...], 0.0), axis=1, keepdims=True)

    def step(masked):
        s = lax.dot_general(q_ref[...], k_ref[...], (((1,), (1,)), ((), ())),
                            preferred_element_type=F32) * scale
        s = s + (ct_ref[...] - crow_ref[0])
        if masked:
            row = lax.broadcasted_iota(jnp.int32, (tq, tk), 0)
            col = lax.broadcasted_iota(jnp.int32, (tq, tk), 1)
            s = jnp.where(col <= row, s, -jnp.inf)
        m_old = m_ref[...]
        m_new = jnp.maximum(m_old, jnp.max(s, axis=1, keepdims=True))
        alpha = jnp.exp(m_old - m_new)
        p = jnp.exp(s - m_new)
        l_ref[...] = alpha * l_ref[...] + jnp.sum(p, axis=1, keepdims=True)
        acc_ref[...] = alpha * acc_ref[...] + jnp.dot(
            p.astype(BF16), v_ref[...], preferred_element_type=F32)
        m_ref[...] = m_new

    @pl.when(j < i)
    def _():
        step(False)

    @pl.when(j == i)
    def _():
        step(True)
        o = acc_ref[...] / l_ref[...]
        y = o * lax.rsqrt(jnp.mean(o * o, axis=-1, keepdims=True) + EPS) * g_ref[...]
        o_ref[...] = y.astype(o_ref.dtype)


def _fox_attention(qkv, cum, cum_t, g_fox, heads):
    t = qkv.shape[0]
    tq = _tile(t, 512)
    nq = t // tq
    pairs = [(i, j) for i in range(nq) for j in range(i + 1)]
    qi = jnp.asarray([p[0] for p in pairs], jnp.int32)
    kj = jnp.asarray([p[1] for p in pairs], jnp.int32)
    kern = functools.partial(_fox_kernel, scale=HEAD_DIM ** -0.5)
    grid_spec = pltpu.PrefetchScalarGridSpec(
        num_scalar_prefetch=2,
        grid=(heads, len(pairs)),
        in_specs=[
            pl.BlockSpec((tq, HEAD_DIM), lambda h, p, qi, kj: (qi[p], h)),
            pl.BlockSpec((tq, HEAD_DIM), lambda h, p, qi, kj: (kj[p], heads + h)),
            pl.BlockSpec((tq, HEAD_DIM), lambda h, p, qi, kj: (kj[p], 2 * heads + h)),
            pl.BlockSpec((tq, LANES), lambda h, p, qi, kj: (qi[p], 0)),
            pl.BlockSpec((1, 1, tq), lambda h, p, qi, kj: (h, 0, kj[p])),
            pl.BlockSpec((1, HEAD_DIM), lambda h, p, qi, kj: (0, h)),
        ],
        out_specs=pl.BlockSpec((tq, HEAD_DIM), lambda h, p, qi, kj: (qi[p], h)),
        scratch_shapes=[pltpu.VMEM((tq, 1), F32), pltpu.VMEM((tq, 1), F32),
                        pltpu.VMEM((tq, HEAD_DIM), F32), pltpu.VMEM((tq, 1), F32)],
    )
    return pl.pallas_call(
        kern,
        out_shape=jax.ShapeDtypeStruct((t, heads * HEAD_DIM), BF16),
        grid_spec=grid_spec,
        compiler_params=_params("parallel", "arbitrary"),
        name="fox_attention",
    )(qi, kj, qkv, qkv, qkv, cum, cum_t, g_fox)


def _hgrn_sum_matrix(c):
    levels = int(math.log2(c))
    p = np.arange(c)[:, None]
    j = np.arange(c)[None, :]
    blocks = []
    for lv in range(levels):
        h = c >> (lv + 1)
        second = ((p // h) % 2) == 1
        m_second = (p // h) * h
        m_first = (p // h + 1) * h
        blocks.append(np.where(second, (j >= m_second) & (j <= p), (j > p) & (j < m_first)))
    blocks.append(j <= p)
    blocks.append(j > p)
    return np.concatenate(blocks, axis=0).astype(np.float32)


def _hgrn_kernel(q_ref, f_ref, i_ref, gate_ref, lbl_ref, g_ref, nmat_ref, o_ref, st_ref, *, chunk, layer):
    tt = q_ref.shape[0]
    levels = int(math.log2(chunk))

    @pl.when(pl.program_id(1) == 0)
    def _():
        st_ref[...] = jnp.zeros_like(st_ref)

    lbl = lbl_ref[...]
    e = jnp.exp(lbl - jnp.max(lbl, axis=0, keepdims=True))
    lb = jnp.sum(e[0:layer + 1, :], axis=0, keepdims=True) / jnp.sum(e, axis=0, keepdims=True)

    rowp = lax.broadcasted_iota(jnp.int32, (chunk, HEAD_DIM), 0)
    tpos = lax.broadcasted_iota(jnp.int32, (chunk, chunk), 0)
    spos = lax.broadcasted_iota(jnp.int32, (chunk, chunk), 1)
    xor = jnp.bitwise_xor(tpos, spos)
    level_of = jnp.where(spos < tpos, 0, -1)
    for lv in range(1, levels):
        level_of = jnp.where((spos < tpos) & (xor < (chunk >> lv)), lv, level_of)
    nmat = nmat_ref[...]

    for c in range(tt // chunk):
        rows = slice(c * chunk, (c + 1) * chunk)
        fg = lb + (1.0 - lb) * jax.nn.sigmoid(f_ref[rows, :])
        logf = jnp.log(fg)
        kk = 1.0 - fg
        qx = q_ref[rows, :]
        qf = qx * jax.nn.sigmoid(qx)
        iv = i_ref[rows, :]

        hi = logf.astype(BF16)
        lo = (logf - hi.astype(F32)).astype(BF16)
        sums = (jnp.dot(nmat, hi, preferred_element_type=F32)
                + jnp.dot(nmat, lo, preferred_element_type=F32))
        dec = jnp.exp(sums)

        a = jnp.where(xor == 0,
                      lax.dot_general(qf.astype(BF16), kk.astype(BF16), (((1,), (1,)), ((), ())),
                                      preferred_element_type=F32), 0.0)
        for lv in range(levels):
            h = chunk >> (lv + 1)
            second = jnp.bitwise_and(rowp, h) != 0
            xl = (jnp.where(second, qf, kk) * dec[lv * chunk:(lv + 1) * chunk, :]).astype(BF16)
            al = lax.dot_general(xl, xl, (((1,), (1,)), ((), ())), preferred_element_type=F32)
            a = jnp.where(level_of == lv, al, a)

        d_inc = dec[levels * chunk:(levels + 1) * chunk, :]
        d_suf = dec[(levels + 1) * chunk:(levels + 2) * chunk, :]
        st = st_ref[...]
        o = jnp.dot(a.astype(BF16), iv, preferred_element_type=F32)
        o = o + lax.dot_general((qf * d_inc).astype(BF16), st.astype(BF16),
                                (((1,), (1,)), ((), ())), preferred_element_type=F32)
        upd = lax.dot_general(iv, (kk * d_suf).astype(BF16), (((0,), (0,)), ((), ())),
                              preferred_element_type=F32)
        st_ref[...] = d_inc[chunk - 1:chunk, :] * st + upd

        y = o * lax.rsqrt(jnp.mean(o * o, axis=-1, keepdims=True) + EPS) * g_ref[...]
        gx = gate_ref[rows, :]
        o_ref[rows, :] = (y * (gx * jax.nn.sigmoid(gx))).astype(o_ref.dtype)


def _hgrn2(qf_proj, i_proj, g_proj, lb_logits, g_out, heads, layer):
    t = i_proj.shape[0]
    chunk = min(HGRN_CHUNK, t)
    tt = _tile(t, 4 * chunk)
    nmat = jnp.asarray(_hgrn_sum_matrix(chunk), dtype=BF16)
    nl = lb_logits.shape[0]
    blk = lambda off: pl.BlockSpec((tt, HEAD_DIM), lambda h, i: (i, off + h))
    return pl.pallas_call(
        functools.partial(_hgrn_kernel, chunk=chunk, layer=layer),
        out_shape=jax.ShapeDtypeStruct((t, heads * HEAD_DIM), BF16),
        grid=(heads, t // tt),
        in_specs=[blk(0), blk(heads), blk(0), blk(0),
                  pl.BlockSpec((nl, HEAD_DIM), lambda h, i: (0, h)),
                  pl.BlockSpec((1, HEAD_DIM), lambda h, i: (0, h)),
                  pl.BlockSpec(nmat.shape, lambda h, i: (0, 0))],
        out_specs=blk(0),
        scratch_shapes=[pltpu.VMEM((HEAD_DIM, HEAD_DIM), F32)],
        compiler_params=_params("parallel", "arbitrary"),
        name="hgrn2",
    )(qf_proj, qf_proj, i_proj, g_proj, lb_logits, g_out, nmat)


def _ffn_up_kernel(h_ref, wa_ref, wv_ref, cwa_ref, cwv_ref, cba_ref, cbv_ref, o_ref,
                   wab_ref, wvb_ref, ua_ref, uv_ref):
    tm = h_ref.shape[0]
    halo = SUBLANES

    @pl.when(pl.program_id(1) == 0)
    def _():
        wab_ref[...] = wa_ref[...].astype(BF16)
        wvb_ref[...] = wv_ref[...].astype(BF16)
        ua_ref[0:halo, :] = jnp.zeros((halo, ua_ref.shape[1]), F32)
        uv_ref[0:halo, :] = jnp.zeros((halo, uv_ref.shape[1]), F32)

    def conv(w_ref, cw_ref, cb_ref, u_ref):
        u = jnp.dot(h_ref[...], w_ref[...], preferred_element_type=F32)
        u_ref[halo:halo + tm, :] = u
        y = cb_ref[...] + cw_ref[0:1, :] * u_ref[halo - 2:halo - 2 + tm, :]
        y = y + cw_ref[1:2, :] * u_ref[halo - 1:halo - 1 + tm, :]
        y = y + cw_ref[2:3, :] * u
        u_ref[0:halo, :] = u_ref[tm:tm + halo, :]
        return y

    ya = conv(wab_ref, cwa_ref, cba_ref, ua_ref)
    yv = conv(wvb_ref, cwv_ref, cbv_ref, uv_ref)
    o_ref[...] = (ya * jax.nn.sigmoid(ya) * yv).astype(o_ref.dtype)


def _ffn_up(h, w_up, conv_w, conv_b):
    t, d = h.shape
    dff = w_up.shape[1] // 2
    tm = _tile(t, 1024)
    tn = _tile(dff, 256)
    nj = dff // tn
    return pl.pallas_call(
        _ffn_up_kernel,
        out_shape=jax.ShapeDtypeStruct((t, dff), BF16),
        grid=(nj, t // tm),
        in_specs=[pl.BlockSpec((tm, d), lambda j, i: (i, 0)),
                  pl.BlockSpec((d, tn), lambda j, i: (0, j)),
                  pl.BlockSpec((d, tn), lambda j, i: (0, nj + j)),
                  pl.BlockSpec((CONV_WIDTH, tn), lambda j, i: (0, j)),
                  pl.BlockSpec((CONV_WIDTH, tn), lambda j, i: (0, nj + j)),
                  pl.BlockSpec((1, tn), lambda j, i: (0, j)),
                  pl.BlockSpec((1, tn), lambda j, i: (0, nj + j))],
        out_specs=pl.BlockSpec((tm, tn), lambda j, i: (i, j)),
        scratch_shapes=[pltpu.VMEM((d, tn), BF16), pltpu.VMEM((d, tn), BF16),
                        pltpu.VMEM((tm + SUBLANES, tn), F32), pltpu.VMEM((tm + SUBLANES, tn), F32)],
        compiler_params=_params("parallel", "arbitrary"),
        name="ffn_up_conv_glu",
    )(h, w_up, w_up, conv_w, conv_w, conv_b, conv_b)


def _ffn_down_kernel(a_ref, w_ref, x_ref, g_ref, o_ref):
    acc = jnp.dot(a_ref[...], w_ref[...], preferred_element_type=F32)
    o_ref[...] = x_ref[...] + g_ref[...] * acc


def _ffn_down(act, w_down_bf16, x, gate):
    t, k = act.shape
    n = w_down_bf16.shape[1]
    tm = _tile(t, 512)
    tn = _tile(n, 256)
    return pl.pallas_call(
        _ffn_down_kernel,
        out_shape=jax.ShapeDtypeStruct((t, n), F32),
        grid=(t // tm, n // tn),
        in_specs=[pl.BlockSpec((tm, k), lambda i, j: (i, 0)),
                  pl.BlockSpec((k, tn), lambda i, j: (0, j)),
                  pl.BlockSpec((tm, tn), lambda i, j: (i, j)),
                  pl.BlockSpec((1, tn), lambda i, j: (0, j))],
        out_specs=pl.BlockSpec((tm, tn), lambda i, j: (i, j)),
        compiler_params=_params("parallel", "parallel"),
        name="ffn_down",
    )(act, w_down_bf16, x, gate)


def kernel(x, c, w_ada, b_ada, g_mix_norm, w_in, b_fox_f, hgrn_lb_logits, g_fox_out, g_hgrn_out,
           w_out, g_ffn_norm, w_up, conv_w, conv_b, w_down, g_final):
    b, t, d = x.shape
    assert b == 1, "single-sequence layer"
    depth = w_ada.shape[0]
    fox_heads = b_fox_f.shape[1]
    fox_width = g_fox_out.shape[1]
    hg_kwidth = hgrn_lb_logits.shape[1]
    hg_vwidth = g_hgrn_out.shape[1]
    hg_heads = hg_vwidth // HEAD_DIM
    assert fox_width == fox_heads * HEAD_DIM and hg_kwidth == hg_heads * HEAD_DIM
    assert fox_heads <= LANES
    fox_f0 = 3 * fox_width
    hg_q0 = fox_f0 + fox_heads

    xs = x.reshape(t, d)
    row = lambda v: v.reshape(1, -1)
    for l in range(depth):
        mod = _ada_mod(c, w_ada[l], b_ada[l])
        sh1, sc1, gt1, sh2, sc2, gt2 = [mod[:, k * d:(k + 1) * d] for k in range(N_MOD)]

        h1 = _norm_mod(xs, row(g_mix_norm[l]), sc1, sh1, BF16)
        w_l = w_in[l]
        w_hg = w_l[:, hg_q0:]
        w_ff = jnp.pad(w_l[:, fox_f0:hg_q0], ((0, 0), (0, LANES - fox_heads)))
        b_ff = jnp.pad(b_fox_f[l], (0, LANES - fox_heads)).reshape(1, LANES)

        qkv = _proj([h1], w_l, 0, 3 * fox_width, BF16, name="proj_fox_qkv")
        ff = _proj([h1], w_ff, 0, LANES, F32, name="proj_fox_f")
        hqf = _proj([h1], w_hg, 0, 2 * hg_kwidth, F32, name="proj_hgrn_qf")
        hi = _proj([h1], w_hg, 2 * hg_kwidth, hg_vwidth, BF16, name="proj_hgrn_i")
        hg = _proj([h1], w_hg, 2 * hg_kwidth + hg_vwidth, hg_vwidth, F32, name="proj_hgrn_g")

        cum = _fox_cum(ff, b_ff)
        cum_t = cum[:, :fox_heads].T.reshape(fox_heads, 1, t)
        o_fox = _fox_attention(qkv, cum, cum_t, row(g_fox_out[l]), fox_heads)
        o_hg = _hgrn2(hqf, hi, hg, hgrn_lb_logits, row(g_hgrn_out[l]), hg_heads, l)

        xs = _proj([o_fox, o_hg], w_out[l], 0, d, F32, res=xs, gate=gt1, name="proj_out")

        h2 = _norm_mod(xs, row(g_ffn_norm[l]), sc2, sh2, BF16)
        act = _ffn_up(h2, w_up[l], conv_w[l], row(conv_b[l]))
        xs = _ffn_down(act, w_down[l].astype(BF16), xs, gt2)

    out = _norm_mod(xs, row(g_final), None, None, x.dtype)
    return out.reshape(b, t, d)
```

```python
import functools
import math

import numpy as np
import jax
import jax.numpy as jnp
from jax import lax
from jax.experimental import pallas as pl
from jax.experimental.pallas import tpu as pltpu

F32 = jnp.float32
BF16 = jnp.bfloat16

EPS = 1e-6
CONV_WIDTH = 3
N_MOD = 6
HEAD_DIM = 128
HGRN_CHUNK = 64
LANES = 128
SUBLANES = 8
VMEM_LIMIT = 56 * 1024 * 1024
LOG2E = 1.4426950408889634


def _params(*sem):
    return pltpu.CompilerParams(dimension_semantics=sem, vmem_limit_bytes=VMEM_LIMIT)


def _tile(n, pref):
    if n <= pref:
        return n
    t = pref
    while n % t:
        t //= 2
    return t


def _ada_kernel(cb_ref, w_ref, b_ref, o_ref, cond_ref):
    @pl.when(pl.program_id(0) == 0)
    def _():
        cb = cb_ref[...]
        cond_ref[...] = cb * jax.nn.sigmoid(cb)

    tn = o_ref.shape[1]
    for j in range(tn // LANES):
        sl = slice(j * LANES, (j + 1) * LANES)
        col = jnp.sum(w_ref[:, sl] * cond_ref[...], axis=0, keepdims=True)
        o_ref[:, sl] = col + b_ref[:, sl]


def _ada_mod(c, w_ada, b_ada):
    d, n = w_ada.shape
    cb = jnp.broadcast_to(c.reshape(d, 1), (d, LANES))
    tn = _tile(n, 512)
    return pl.pallas_call(
        _ada_kernel,
        out_shape=jax.ShapeDtypeStruct((1, n), F32),
        grid=(n // tn,),
        in_specs=[pl.BlockSpec((d, LANES), lambda j: (0, 0)),
                  pl.BlockSpec((d, tn), lambda j: (0, j)),
                  pl.BlockSpec((1, tn), lambda j: (0, j))],
        out_specs=pl.BlockSpec((1, tn), lambda j: (0, j)),
        scratch_shapes=[pltpu.VMEM((d, LANES), F32)],
        compiler_params=_params("arbitrary"),
        name="ada_mod",
    )(cb, w_ada, b_ada.reshape(1, n))


def _norm_mod_kernel(x_ref, g_ref, sc_ref, sh_ref, o_ref):
    x = x_ref[...]
    y = x * lax.rsqrt(jnp.mean(x * x, axis=-1, keepdims=True) + EPS) * g_ref[...]
    o_ref[...] = (y * (1.0 + sc_ref[...]) + sh_ref[...]).astype(o_ref.dtype)


def _norm_kernel(x_ref, g_ref, o_ref):
    x = x_ref[...]
    y = x * lax.rsqrt(jnp.mean(x * x, axis=-1, keepdims=True) + EPS) * g_ref[...]
    o_ref[...] = y.astype(o_ref.dtype)


def _norm_mod(x, g, sc, sh, out_dtype):
    t, d = x.shape
    tm = _tile(t, 256)
    row = pl.BlockSpec((tm, d), lambda i: (i, 0))
    vec = pl.BlockSpec((1, d), lambda i: (0, 0))
    if sc is None:
        kern, args, specs = _norm_kernel, (x, g), [row, vec]
    else:
        kern, args, specs = _norm_mod_kernel, (x, g, sc, sh), [row, vec, vec, vec]
    return pl.pallas_call(
        kern,
        out_shape=jax.ShapeDtypeStruct((t, d), out_dtype),
        grid=(t // tm,),
        in_specs=specs,
        out_specs=row,
        compiler_params=_params("parallel"),
        name="rms_norm",
    )(*args)


def _proj_kernel(*refs, n_a, has_res, lead_tiles, lead_scale, lane_shift):
    a_refs = refs[:n_a]
    w_ref = refs[n_a]
    pos = n_a + 1
    if lane_shift:
        wx_ref = refs[pos]
        pos += 1
    if has_res:
        x_ref, g_ref = refs[pos], refs[pos + 1]
        pos += 2
    o_ref, wb_ref = refs[pos], refs[pos + 1]
    k, tn = wb_ref.shape

    @pl.when(pl.program_id(1) == 0)
    def _():
        if lane_shift:
            rows = _tile(k, 512)
            for r in range(0, k, rows):
                win = jnp.concatenate([w_ref[r:r + rows, :], wx_ref[r:r + rows, :]], axis=1)
                wb_ref[r:r + rows, :] = win[:, lane_shift:lane_shift + tn].astype(BF16)
        else:
            wb_ref[...] = w_ref[...].astype(BF16)

    acc = None
    k0 = 0
    for a_ref in a_refs:
        ks = a_ref.shape[1]
        part = jnp.dot(a_ref[...], wb_ref[k0:k0 + ks, :], preferred_element_type=F32)
        acc = part if acc is None else acc + part
        k0 += ks
    if lead_tiles:
        acc = acc * jnp.where(pl.program_id(0) < lead_tiles, lead_scale, 1.0).astype(F32)
    if has_res:
        acc = x_ref[...] + g_ref[...] * acc
    o_ref[...] = acc.astype(o_ref.dtype)


def _proj(a_list, w, col0, n, out_dtype, res=None, gate=None, lead_cols=0, lead_scale=1.0,
          lane_shift=0, name="proj"):
    m = a_list[0].shape[0]
    k = w.shape[0]
    assert sum(a.shape[1] for a in a_list) == k
    tm = _tile(m, 1024)
    tn = _tile(math.gcd(n, col0) if col0 else n, 512)
    assert tn % LANES == 0 and lead_cols % tn == 0 and 0 <= lane_shift < LANES
    cb = col0 // tn
    in_specs = [pl.BlockSpec((tm, a.shape[1]), lambda j, i: (i, 0)) for a in a_list]
    in_specs.append(pl.BlockSpec((k, tn), lambda j, i: (0, cb + j)))
    args = list(a_list) + [w]
    if lane_shift:
        lanes_per_tile = tn // LANES
        in_specs.append(pl.BlockSpec((k, LANES), lambda j, i: (0, (cb + j + 1) * lanes_per_tile)))
        args.append(w)
    if res is not None:
        in_specs += [pl.BlockSpec((tm, tn), lambda j, i: (i, j)),
                     pl.BlockSpec((1, tn), lambda j, i: (0, j))]
        args += [res, gate]
    return pl.pallas_call(
        functools.partial(_proj_kernel, n_a=len(a_list), has_res=res is not None,
                          lead_tiles=lead_cols // tn, lead_scale=lead_scale, lane_shift=lane_shift),
        out_shape=jax.ShapeDtypeStruct((m, n), out_dtype),
        grid=(n // tn, m // tm),
        in_specs=in_specs,
        out_specs=pl.BlockSpec((tm, tn), lambda j, i: (i, j)),
        scratch_shapes=[pltpu.VMEM((k, tn), BF16)],
        compiler_params=_params("parallel", "arbitrary"),
        name=name,
    )(*args)


def _split3(x):
    hi = x.astype(BF16)
    r1 = x - hi.astype(F32)
    mid = r1.astype(BF16)
    lo = (r1 - mid.astype(F32)).astype(BF16)
    return hi, mid, lo


def _cum_kernel(ff_ref, b_ref, qa_ref, ka_ref, carry_ref, *, heads):
    @pl.when(pl.program_id(0) == 0)
    def _():
        carry_ref[...] = jnp.zeros_like(carry_ref)

    z = ff_ref[...] + b_ref[...]
    logf = jnp.minimum(z, 0.0) - jnp.log1p(jnp.exp(-jnp.abs(z)))
    tt = z.shape[0]
    row = lax.broadcasted_iota(jnp.int32, (tt, tt), 0)
    col = lax.broadcasted_iota(jnp.int32, (tt, tt), 1)
    tril = jnp.where(col <= row, 1.0, 0.0).astype(BF16)
    local = None
    for piece in _split3(logf):
        part = jnp.dot(tril, piece, preferred_element_type=F32)
        local = part if local is None else local + part
    cum = local + carry_ref[0:1, :]
    carry_ref[...] = jnp.broadcast_to(cum[tt - 1:tt, :], carry_ref.shape)

    c2 = cum * LOG2E
    lane = lax.broadcasted_iota(jnp.int32, (tt, LANES), 1)
    for h in range(heads):
        hi, mid, lo = [p.astype(F32) for p in _split3(c2[:, h:h + 1])]
        qa = jnp.where(lane == 0, hi, jnp.where(lane == 1, mid, jnp.where(lane == 2, lo,
                       jnp.where(lane < 6, 1.0, 0.0))))
        ka = jnp.where(lane < 3, 1.0, jnp.where(lane == 3, -hi, jnp.where(lane == 4, -mid,
                       jnp.where(lane == 5, -lo, 0.0))))
        qa_ref[:, h * LANES:(h + 1) * LANES] = qa.astype(BF16)
        ka_ref[:, h * LANES:(h + 1) * LANES] = ka.astype(BF16)


def _fox_bias_terms(ff, b_pad, heads):
    t, n = ff.shape
    tt = _tile(t, 256)
    out = jax.ShapeDtypeStruct((t, heads * LANES), BF16)
    return pl.pallas_call(
        functools.partial(_cum_kernel, heads=heads),
        out_shape=(out, out),
        grid=(t // tt,),
        in_specs=[pl.BlockSpec((tt, n), lambda i: (i, 0)),
                  pl.BlockSpec((1, n), lambda i: (0, 0))],
        out_specs=(pl.BlockSpec((tt, heads * LANES), lambda i: (i, 0)),
                   pl.BlockSpec((tt, heads * LANES), lambda i: (i, 0))),
        scratch_shapes=[pltpu.VMEM((SUBLANES, n), F32)],
        compiler_params=_params("arbitrary"),
        name="fox_bias_terms",
    )(ff, b_pad)


V_ROWS = HEAD_DIM + 16


def _fox_kernel(q_ref, qa_ref, k_ref, ka_ref, vt_ref, g_ref, o_ref, qc_ref, m_ref, acc_ref):
    i = pl.program_id(1)
    tq = q_ref.shape[0]
    tk = tq
    hp = vt_ref.shape[0]
    cols = lambda hh: slice(hh * HEAD_DIM, (hh + 1) * HEAD_DIM)

    for hh in range(hp):
        qc_ref[hh, :, 0:HEAD_DIM] = q_ref[:, cols(hh)]
        qc_ref[hh, :, HEAD_DIM:2 * HEAD_DIM] = qa_ref[:, cols(hh)]
    m_ref[...] = jnp.full_like(m_ref, -jnp.inf)
    acc_ref[...] = jnp.zeros_like(acc_ref)

    def tile(hh, j, masked):
        k0 = pl.multiple_of(j * tk, tk)
        kc = jnp.concatenate([k_ref[pl.ds(k0, tk), cols(hh)], ka_ref[pl.ds(k0, tk), cols(hh)]], axis=1)
        s = lax.dot_general(kc, qc_ref[hh], (((1,), (1,)), ((), ())),
                            preferred_element_type=F32)
        if masked:
            key = lax.broadcasted_iota(jnp.int32, (tk, tq), 0)
            qry = lax.broadcasted_iota(jnp.int32, (tk, tq), 1)
            s = jnp.where(key <= qry, s, -jnp.inf)
        m_old = m_ref[hh]
        m_new = jnp.maximum(m_old, jnp.max(s, axis=0, keepdims=True))
        alpha = jnp.exp2(m_old - m_new)
        p = jnp.exp2(s - m_new).astype(BF16)
        pv = jnp.dot(vt_ref[hh, :, pl.ds(k0, tk)], p, preferred_element_type=F32)
        acc_ref[hh] = alpha * acc_ref[hh] + pv
        m_ref[hh] = m_new

    def body(j, carry):
        for hh in range(hp):
            tile(hh, j, False)
        return carry

    lax.fori_loop(0, i, body, 0)
    for hh in range(hp):
        tile(hh, i, True)

    for hh in range(hp):
        acc = acc_ref[hh]
        o = (acc[0:HEAD_DIM, :] / acc[HEAD_DIM:HEAD_DIM + 1, :]).T
        y = o * lax.rsqrt(jnp.mean(o * o, axis=-1, keepdims=True) + EPS) * g_ref[:, cols(hh)]
        o_ref[:, cols(hh)] = y.astype(o_ref.dtype)


def _fox_attention(qkv, qa, ka, vt, g_fox, heads):
    t = qkv.shape[0]
    tq = _tile(t, 512)
    hp = 2 if heads % 2 == 0 else 1
    w = hp * HEAD_DIM
    groups = heads // hp
    per_group = lambda off: pl.BlockSpec((t, w), lambda h, i: (0, off + h))
    q_tile = pl.BlockSpec((tq, w), lambda h, i: (i, h))
    return pl.pallas_call(
        _fox_kernel,
        out_shape=jax.ShapeDtypeStruct((t, heads * HEAD_DIM), BF16),
        grid=(groups, t // tq),
        in_specs=[q_tile, q_tile, per_group(groups), per_group(0),
                  pl.BlockSpec((hp, V_ROWS, t), lambda h, i: (h, 0, 0)),
                  pl.BlockSpec((1, w), lambda h, i: (0, h))],
        out_specs=q_tile,
        scratch_shapes=[pltpu.VMEM((hp, tq, 2 * HEAD_DIM), BF16), pltpu.VMEM((hp, 1, tq), F32),
                        pltpu.VMEM((hp, V_ROWS, tq), F32)],
        compiler_params=_params("parallel", "arbitrary"),
        name="fox_attention",
    )(qkv, qa, qkv, ka, vt, g_fox)


def _hgrn_sum_matrix(c):
    levels = int(math.log2(c))
    p = np.arange(c)[:, None]
    j = np.arange(c)[None, :]
    blocks = []
    for lv in range(levels):
        h = c >> (lv + 1)
        second = ((p // h) % 2) == 1
        m_second = (p // h) * h
        m_first = (p // h + 1) * h
        blocks.append(np.where(second, (j >= m_second) & (j <= p), (j > p) & (j < m_first)))
    blocks.append(j <= p)
    blocks.append(j > p)
    return np.concatenate(blocks, axis=0).astype(np.float32)


def _hgrn_kernel(q_ref, f_ref, i_ref, gate_ref, lbl_ref, g_ref, nmat_ref, o_ref, st_ref, *, chunk, layer):
    tt = q_ref.shape[0]
    levels = int(math.log2(chunk))

    @pl.when(pl.program_id(1) == 0)
    def _():
        st_ref[...] = jnp.zeros_like(st_ref)

    lbl = lbl_ref[...]
    e = jnp.exp(lbl - jnp.max(lbl, axis=0, keepdims=True))
    lb = jnp.sum(e[0:layer + 1, :], axis=0, keepdims=True) / jnp.sum(e, axis=0, keepdims=True)

    rowp = lax.broadcasted_iota(jnp.int32, (chunk, HEAD_DIM), 0)
    tpos = lax.broadcasted_iota(jnp.int32, (chunk, chunk), 0)
    spos = lax.broadcasted_iota(jnp.int32, (chunk, chunk), 1)
    xor = jnp.bitwise_xor(tpos, spos)
    level_of = jnp.where(spos < tpos, 0, -1)
    for lv in range(1, levels):
        level_of = jnp.where((spos < tpos) & (xor < (chunk >> lv)), lv, level_of)
    nmat = nmat_ref[...]

    for c in range(tt // chunk):
        rows = slice(c * chunk, (c + 1) * chunk)
        fg = lb + (1.0 - lb) * jax.nn.sigmoid(f_ref[rows, :])
        logf = jnp.log(fg)
        kk = 1.0 - fg
        qx = q_ref[rows, :]
        qf = qx * jax.nn.sigmoid(qx)
        iv = i_ref[rows, :]

        hi = logf.astype(BF16)
        lo = (logf - hi.astype(F32)).astype(BF16)
        sums = (jnp.dot(nmat, hi, preferred_element_type=F32)
                + jnp.dot(nmat, lo, preferred_element_type=F32))
        dec = jnp.exp(sums)

        a = jnp.where(xor == 0,
                      lax.dot_general(qf.astype(BF16), kk.astype(BF16), (((1,), (1,)), ((), ())),
                                      preferred_element_type=F32), 0.0)
        for lv in range(levels):
            h = chunk >> (lv + 1)
            second = jnp.bitwise_and(rowp, h) != 0
            xl = (jnp.where(second, qf, kk) * dec[lv * chunk:(lv + 1) * chunk, :]).astype(BF16)
            al = lax.dot_general(xl, xl, (((1,), (1,)), ((), ())), preferred_element_type=F32)
            a = jnp.where(level_of == lv, al, a)

        d_inc = dec[levels * chunk:(levels + 1) * chunk, :]
        d_suf = dec[(levels + 1) * chunk:(levels + 2) * chunk, :]
        st = st_ref[...]
        o = jnp.dot(a.astype(BF16), iv, preferred_element_type=F32)
        o = o + lax.dot_general((qf * d_inc).astype(BF16), st.astype(BF16),
                                (((1,), (1,)), ((), ())), preferred_element_type=F32)
        upd = lax.dot_general(iv, (kk * d_suf).astype(BF16), (((0,), (0,)), ((), ())),
                              preferred_element_type=F32)
        st_ref[...] = d_inc[chunk - 1:chunk, :] * st + upd

        y = o * lax.rsqrt(jnp.mean(o * o, axis=-1, keepdims=True) + EPS) * g_ref[...]
        gx = gate_ref[rows, :]
        o_ref[rows, :] = (y * (gx * jax.nn.sigmoid(gx))).astype(o_ref.dtype)


def _hgrn2(qf_proj, i_proj, g_proj, lb_logits, g_out, heads, layer):
    t = i_proj.shape[0]
    chunk = min(HGRN_CHUNK, t)
    tt = _tile(t, 4 * chunk)
    nmat = jnp.asarray(_hgrn_sum_matrix(chunk), dtype=BF16)
    nl = lb_logits.shape[0]
    blk = lambda off: pl.BlockSpec((tt, HEAD_DIM), lambda h, i: (i, off + h))
    return pl.pallas_call(
        functools.partial(_hgrn_kernel, chunk=chunk, layer=layer),
        out_shape=jax.ShapeDtypeStruct((t, heads * HEAD_DIM), BF16),
        grid=(heads, t // tt),
        in_specs=[blk(0), blk(heads), blk(0), blk(0),
                  pl.BlockSpec((nl, HEAD_DIM), lambda h, i: (0, h)),
                  pl.BlockSpec((1, HEAD_DIM), lambda h, i: (0, h)),
                  pl.BlockSpec(nmat.shape, lambda h, i: (0, 0))],
        out_specs=blk(0),
        scratch_shapes=[pltpu.VMEM((HEAD_DIM, HEAD_DIM), F32)],
        compiler_params=_params("parallel", "arbitrary"),
        name="hgrn2",
    )(qf_proj, qf_proj, i_proj, g_proj, lb_logits, g_out, nmat)


def _ffn_up_kernel(h_ref, wa_ref, wv_ref, cwa_ref, cwv_ref, cba_ref, cbv_ref, o_ref,
                   wab_ref, wvb_ref, ua_ref, uv_ref):
    tm = h_ref.shape[0]
    halo = SUBLANES

    @pl.when(pl.program_id(1) == 0)
    def _():
        wab_ref[...] = wa_ref[...].astype(BF16)
        wvb_ref[...] = wv_ref[...].astype(BF16)
        ua_ref[0:halo, :] = jnp.zeros((halo, ua_ref.shape[1]), F32)
        uv_ref[0:halo, :] = jnp.zeros((halo, uv_ref.shape[1]), F32)

    def conv(w_ref, cw_ref, cb_ref, u_ref):
        u = jnp.dot(h_ref[...], w_ref[...], preferred_element_type=F32)
        u_ref[halo:halo + tm, :] = u
        y = cb_ref[...] + cw_ref[0:1, :] * u_ref[halo - 2:halo - 2 + tm, :]
        y = y + cw_ref[1:2, :] * u_ref[halo - 1:halo - 1 + tm, :]
        y = y + cw_ref[2:3, :] * u
        u_ref[0:halo, :] = u_ref[tm:tm + halo, :]
        return y

    ya = conv(wab_ref, cwa_ref, cba_ref, ua_ref)
    yv = conv(wvb_ref, cwv_ref, cbv_ref, uv_ref)
    o_ref[...] = (ya * jax.nn.sigmoid(ya) * yv).astype(o_ref.dtype)


def _ffn_up(h, w_up, conv_w, conv_b):
    t, d = h.shape
    dff = w_up.shape[1] // 2
    tm = _tile(t, 1024)
    tn = _tile(dff, 256)
    nj = dff // tn
    return pl.pallas_call(
        _ffn_up_kernel,
        out_shape=jax.ShapeDtypeStruct((t, dff), BF16),
        grid=(nj, t // tm),
        in_specs=[pl.BlockSpec((tm, d), lambda j, i: (i, 0)),
                  pl.BlockSpec((d, tn), lambda j, i: (0, j)),
                  pl.BlockSpec((d, tn), lambda j, i: (0, nj + j)),
                  pl.BlockSpec((CONV_WIDTH, tn), lambda j, i: (0, j)),
                  pl.BlockSpec((CONV_WIDTH, tn), lambda j, i: (0, nj + j)),
                  pl.BlockSpec((1, tn), lambda j, i: (0, j)),
                  pl.BlockSpec((1, tn), lambda j, i: (0, nj + j))],
        out_specs=pl.BlockSpec((tm, tn), lambda j, i: (i, j)),
        scratch_shapes=[pltpu.VMEM((d, tn), BF16), pltpu.VMEM((d, tn), BF16),
                        pltpu.VMEM((tm + SUBLANES, tn), F32), pltpu.VMEM((tm + SUBLANES, tn), F32)],
        compiler_params=_params("parallel", "arbitrary"),
        name="ffn_up_conv_glu",
    )(h, w_up, w_up, conv_w, conv_w, conv_b, conv_b)


def _ffn_down_kernel(a_ref, w_ref, x_ref, g_ref, o_ref):
    acc = jnp.dot(a_ref[...], w_ref[...], preferred_element_type=F32)
    o_ref[...] = x_ref[...] + g_ref[...] * acc


def _ffn_down(act, w_down_bf16, x, gate):
    t, k = act.shape
    n = w_down_bf16.shape[1]
    tm = _tile(t, 512)
    tn = _tile(n, 256)
    return pl.pallas_call(
        _ffn_down_kernel,
        out_shape=jax.ShapeDtypeStruct((t, n), F32),
        grid=(t // tm, n // tn),
        in_specs=[pl.BlockSpec((tm, k), lambda i, j: (i, 0)),
                  pl.BlockSpec((k, tn), lambda i, j: (0, j)),
                  pl.BlockSpec((tm, tn), lambda i, j: (i, j)),
                  pl.BlockSpec((1, tn), lambda i, j: (0, j))],
        out_specs=pl.BlockSpec((tm, tn), lambda i, j: (i, j)),
        compiler_params=_params("parallel", "parallel"),
        name="ffn_down",
    )(act, w_down_bf16, x, gate)


def kernel(x, c, w_ada, b_ada, g_mix_norm, w_in, b_fox_f, hgrn_lb_logits, g_fox_out, g_hgrn_out,
           w_out, g_ffn_norm, w_up, conv_w, conv_b, w_down, g_final):
    b, t, d = x.shape
    assert b == 1, "single-sequence layer"
    depth = w_ada.shape[0]
    fox_heads = b_fox_f.shape[1]
    fox_width = g_fox_out.shape[1]
    hg_kwidth = hgrn_lb_logits.shape[1]
    hg_vwidth = g_hgrn_out.shape[1]
    hg_heads = hg_vwidth // HEAD_DIM
    assert fox_width == fox_heads * HEAD_DIM and hg_kwidth == hg_heads * HEAD_DIM
    assert fox_heads <= LANES
    fox_f0 = 3 * fox_width
    hg_q0 = fox_f0 + fox_heads

    xs = x.reshape(t, d)
    row = lambda v: v.reshape(1, -1)
    for l in range(depth):
        mod = _ada_mod(c, w_ada[l], b_ada[l])
        sh1, sc1, gt1, sh2, sc2, gt2 = [mod[:, k * d:(k + 1) * d] for k in range(N_MOD)]

        h1 = _norm_mod(xs, row(g_mix_norm[l]), sc1, sh1, BF16)
        w_l = w_in[l]
        hg0 = hg_q0 - hg_q0 % LANES
        shift = hg_q0 % LANES
        b_ff = jnp.pad(b_fox_f[l], (0, LANES - fox_heads)).reshape(1, LANES)

        qkv = _proj([h1], w_l, 0, 3 * fox_width, BF16, lead_cols=fox_width,
                    lead_scale=LOG2E * HEAD_DIM ** -0.5, name="proj_fox_qkv")
        ff = _proj([h1], w_l, fox_f0, LANES, F32, name="proj_fox_f")
        hqf = _proj([h1], w_l, hg0, 2 * hg_kwidth, F32, lane_shift=shift, name="proj_hgrn_qf")
        hi = _proj([h1], w_l, hg0 + 2 * hg_kwidth, hg_vwidth, BF16, lane_shift=shift, name="proj_hgrn_i")
        hg = _proj([h1], w_l, hg0 + 2 * hg_kwidth + hg_vwidth, hg_vwidth, F32, lane_shift=shift,
                   name="proj_hgrn_g")

        qa, ka = _fox_bias_terms(ff, b_ff, fox_heads)
        vt = qkv[:, 2 * fox_width:].reshape(t, fox_heads, HEAD_DIM).transpose(1, 2, 0)
        vt = jnp.concatenate([vt, jnp.ones((fox_heads, V_ROWS - HEAD_DIM, t), BF16)], axis=1)
        o_fox = _fox_attention(qkv, qa, ka, vt, row(g_fox_out[l]), fox_heads)
        o_hg = _hgrn2(hqf, hi, hg, hgrn_lb_logits, row(g_hgrn_out[l]), hg_heads, l)

        xs = _proj([o_fox, o_hg], w_out[l], 0, d, F32, res=xs, gate=gt1, name="proj_out")

        h2 = _norm_mod(xs, row(g_ffn_norm[l]), sc2, sh2, BF16)
        act = _ffn_up(h2, w_up[l], conv_w[l], row(conv_b[l]))
        xs = _ffn_down(act, w_down[l].astype(BF16), xs, gt2)

    out = _norm_mod(xs, row(g_final), None, None, x.dtype)
    return out.reshape(b, t, d)
```

```python
import functools
import math

import numpy as np
import jax
import jax.numpy as jnp
from jax import lax
from jax.experimental import pallas as pl
from jax.experimental.pallas import tpu as pltpu

F32 = jnp.float32
BF16 = jnp.bfloat16

EPS = 1e-6
CONV_WIDTH = 3
N_MOD = 6
HEAD_DIM = 128
HGRN_CHUNK = 64
LANES = 128
SUBLANES = 8
VMEM_LIMIT = 56 * 1024 * 1024
LOG2E = 1.4426950408889634


def _params(*sem):
    return pltpu.CompilerParams(dimension_semantics=sem, vmem_limit_bytes=VMEM_LIMIT)


def _tile(n, pref):
    if n <= pref:
        return n
    t = pref
    while n % t:
        t //= 2
    return t


def _ada_kernel(cb_ref, w_ref, b_ref, o_ref, cond_ref):
    @pl.when(pl.program_id(0) == 0)
    def _():
        cb = cb_ref[...]
        cond_ref[...] = cb * jax.nn.sigmoid(cb)

    tn = o_ref.shape[1]
    for j in range(tn // LANES):
        sl = slice(j * LANES, (j + 1) * LANES)
        col = jnp.sum(w_ref[:, sl] * cond_ref[...], axis=0, keepdims=True)
        o_ref[:, sl] = col + b_ref[:, sl]


def _ada_mod(c, w_ada, b_ada):
    d, n = w_ada.shape
    cb = jnp.broadcast_to(c.reshape(d, 1), (d, LANES))
    tn = _tile(n, 512)
    return pl.pallas_call(
        _ada_kernel,
        out_shape=jax.ShapeDtypeStruct((1, n), F32),
        grid=(n // tn,),
        in_specs=[pl.BlockSpec((d, LANES), lambda j: (0, 0)),
                  pl.BlockSpec((d, tn), lambda j: (0, j)),
                  pl.BlockSpec((1, tn), lambda j: (0, j))],
        out_specs=pl.BlockSpec((1, tn), lambda j: (0, j)),
        scratch_shapes=[pltpu.VMEM((d, LANES), F32)],
        compiler_params=_params("arbitrary"),
        name="ada_mod",
    )(cb, w_ada, b_ada.reshape(1, n))


def _norm_mod_kernel(x_ref, g_ref, sc_ref, sh_ref, o_ref):
    x = x_ref[...]
    y = x * lax.rsqrt(jnp.mean(x * x, axis=-1, keepdims=True) + EPS) * g_ref[...]
    o_ref[...] = (y * (1.0 + sc_ref[...]) + sh_ref[...]).astype(o_ref.dtype)


def _norm_kernel(x_ref, g_ref, o_ref):
    x = x_ref[...]
    y = x * lax.rsqrt(jnp.mean(x * x, axis=-1, keepdims=True) + EPS) * g_ref[...]
    o_ref[...] = y.astype(o_ref.dtype)


def _norm_mod(x, g, sc, sh, out_dtype):
    t, d = x.shape
    tm = _tile(t, 256)
    row = pl.BlockSpec((tm, d), lambda i: (i, 0))
    vec = pl.BlockSpec((1, d), lambda i: (0, 0))
    if sc is None:
        kern, args, specs = _norm_kernel, (x, g), [row, vec]
    else:
        kern, args, specs = _norm_mod_kernel, (x, g, sc, sh), [row, vec, vec, vec]
    return pl.pallas_call(
        kern,
        out_shape=jax.ShapeDtypeStruct((t, d), out_dtype),
        grid=(t // tm,),
        in_specs=specs,
        out_specs=row,
        compiler_params=_params("parallel"),
        name="rms_norm",
    )(*args)


def _proj_kernel(*refs, n_a, has_res, lead_tiles, lead_scale, transposed):
    a_refs = refs[:n_a]
    w_ref = refs[n_a]
    pos = n_a + 1
    if has_res:
        x_ref, g_ref = refs[pos], refs[pos + 1]
        pos += 2
    o_ref, wb_ref = refs[pos], refs[pos + 1]

    @pl.when(pl.program_id(1) == 0)
    def _():
        wb_ref[...] = w_ref[...].astype(BF16)

    acc = None
    k0 = 0
    for a_ref in a_refs:
        ks = a_ref.shape[1]
        if transposed:
            part = lax.dot_general(a_ref[...], wb_ref[:, k0:k0 + ks], (((1,), (1,)), ((), ())),
                                   preferred_element_type=F32)
        else:
            part = jnp.dot(a_ref[...], wb_ref[k0:k0 + ks, :], preferred_element_type=F32)
        acc = part if acc is None else acc + part
        k0 += ks
    if lead_tiles:
        acc = acc * jnp.where(pl.program_id(0) < lead_tiles, lead_scale, 1.0).astype(F32)
    if has_res:
        acc = x_ref[...] + g_ref[...] * acc
    o_ref[...] = acc.astype(o_ref.dtype)


def _proj(a_list, w, col0, n, out_dtype, res=None, gate=None, lead_cols=0, lead_scale=1.0,
          transposed=False, name="proj"):
    m = a_list[0].shape[0]
    k = w.shape[1] if transposed else w.shape[0]
    assert sum(a.shape[1] for a in a_list) == k
    tm = _tile(m, 1024)
    tn = _tile(n, 512)
    assert lead_cols % tn == 0
    in_specs = [pl.BlockSpec((tm, a.shape[1]), lambda j, i: (i, 0)) for a in a_list]
    if transposed:
        assert col0 % SUBLANES == 0
        in_specs.append(pl.BlockSpec((pl.Element(tn), pl.Element(k)),
                                     lambda j, i: (pl.multiple_of(col0 + j * tn, SUBLANES), 0)))
        wb_shape = (tn, k)
    else:
        assert col0 % tn == 0
        in_specs.append(pl.BlockSpec((k, tn), lambda j, i: (0, col0 // tn + j)))
        wb_shape = (k, tn)
    args = list(a_list) + [w]
    if res is not None:
        in_specs += [pl.BlockSpec((tm, tn), lambda j, i: (i, j)),
                     pl.BlockSpec((1, tn), lambda j, i: (0, j))]
        args += [res, gate]
    return pl.pallas_call(
        functools.partial(_proj_kernel, n_a=len(a_list), has_res=res is not None,
                          lead_tiles=lead_cols // tn, lead_scale=lead_scale, transposed=transposed),
        out_shape=jax.ShapeDtypeStruct((m, n), out_dtype),
        grid=(n // tn, m // tm),
        in_specs=in_specs,
        out_specs=pl.BlockSpec((tm, tn), lambda j, i: (i, j)),
        scratch_shapes=[pltpu.VMEM(wb_shape, BF16)],
        compiler_params=_params("parallel", "arbitrary"),
        name=name,
    )(*args)


def _split3(x):
    hi = x.astype(BF16)
    r1 = x - hi.astype(F32)
    mid = r1.astype(BF16)
    lo = (r1 - mid.astype(F32)).astype(BF16)
    return hi, mid, lo


def _cum_kernel(ff_ref, b_ref, qa_ref, ka_ref, carry_ref, *, heads):
    @pl.when(pl.program_id(0) == 0)
    def _():
        carry_ref[...] = jnp.zeros_like(carry_ref)

    z = ff_ref[...] + b_ref[...]
    logf = jnp.minimum(z, 0.0) - jnp.log1p(jnp.exp(-jnp.abs(z)))
    tt = z.shape[0]
    row = lax.broadcasted_iota(jnp.int32, (tt, tt), 0)
    col = lax.broadcasted_iota(jnp.int32, (tt, tt), 1)
    tril = jnp.where(col <= row, 1.0, 0.0).astype(BF16)
    local = None
    for piece in _split3(logf):
        part = jnp.dot(tril, piece, preferred_element_type=F32)
        local = part if local is None else local + part
    cum = local + carry_ref[0:1, :]
    carry_ref[...] = jnp.broadcast_to(cum[tt - 1:tt, :], carry_ref.shape)

    c2 = cum * LOG2E
    lane = lax.broadcasted_iota(jnp.int32, (tt, LANES), 1)
    for h in range(heads):
        hi, mid, lo = [p.astype(F32) for p in _split3(c2[:, h:h + 1])]
        qa = jnp.where(lane == 0, hi, jnp.where(lane == 1, mid, jnp.where(lane == 2, lo,
                       jnp.where(lane < 6, 1.0, 0.0))))
        ka = jnp.where(lane < 3, 1.0, jnp.where(lane == 3, -hi, jnp.where(lane == 4, -mid,
                       jnp.where(lane == 5, -lo, 0.0))))
        qa_ref[:, h * LANES:(h + 1) * LANES] = qa.astype(BF16)
        ka_ref[:, h * LANES:(h + 1) * LANES] = ka.astype(BF16)


def _fox_bias_terms(ff, b_pad, heads):
    t, n = ff.shape
    tt = _tile(t, 256)
    out = jax.ShapeDtypeStruct((t, heads * LANES), BF16)
    return pl.pallas_call(
        functools.partial(_cum_kernel, heads=heads),
        out_shape=(out, out),
        grid=(t // tt,),
        in_specs=[pl.BlockSpec((tt, n), lambda i: (i, 0)),
                  pl.BlockSpec((1, n), lambda i: (0, 0))],
        out_specs=(pl.BlockSpec((tt, heads * LANES), lambda i: (i, 0)),
                   pl.BlockSpec((tt, heads * LANES), lambda i: (i, 0))),
        scratch_shapes=[pltpu.VMEM((SUBLANES, n), F32)],
        compiler_params=_params("arbitrary"),
        name="fox_bias_terms",
    )(ff, b_pad)


V_ROWS = HEAD_DIM + 16


def _fox_kernel(q_ref, qa_ref, k_ref, ka_ref, vt_ref, g_ref, o_ref, qc_ref, m_ref, acc_ref):
    i = pl.program_id(1)
    tq = q_ref.shape[0]
    tk = tq
    hp = vt_ref.shape[0]
    cols = lambda hh: slice(hh * HEAD_DIM, (hh + 1) * HEAD_DIM)

    for hh in range(hp):
        qc_ref[hh, :, 0:HEAD_DIM] = q_ref[:, cols(hh)]
        qc_ref[hh, :, HEAD_DIM:2 * HEAD_DIM] = qa_ref[:, cols(hh)]
    m_ref[...] = jnp.full_like(m_ref, -jnp.inf)
    acc_ref[...] = jnp.zeros_like(acc_ref)

    def tile(hh, j, masked):
        k0 = pl.multiple_of(j * tk, tk)
        kc = jnp.concatenate([k_ref[pl.ds(k0, tk), cols(hh)], ka_ref[pl.ds(k0, tk), cols(hh)]], axis=1)
        s = lax.dot_general(kc, qc_ref[hh], (((1,), (1,)), ((), ())),
                            preferred_element_type=F32)
        if masked:
            key = lax.broadcasted_iota(jnp.int32, (tk, tq), 0)
            qry = lax.broadcasted_iota(jnp.int32, (tk, tq), 1)
            s = jnp.where(key <= qry, s, -jnp.inf)
        m_old = m_ref[hh]
        m_new = jnp.maximum(m_old, jnp.max(s, axis=0, keepdims=True))
        alpha = jnp.exp2(m_old - m_new)
        p = jnp.exp2(s - m_new).astype(BF16)
        pv = jnp.dot(vt_ref[hh, :, pl.ds(k0, tk)], p, preferred_element_type=F32)
        acc_ref[hh] = alpha * acc_ref[hh] + pv
        m_ref[hh] = m_new

    def body(j, carry):
        for hh in range(hp):
            tile(hh, j, False)
        return carry

    lax.fori_loop(0, i, body, 0)
    for hh in range(hp):
        tile(hh, i, True)

    for hh in range(hp):
        acc = acc_ref[hh]
        o = (acc[0:HEAD_DIM, :] / acc[HEAD_DIM:HEAD_DIM + 1, :]).T
        y = o * lax.rsqrt(jnp.mean(o * o, axis=-1, keepdims=True) + EPS) * g_ref[:, cols(hh)]
        o_ref[:, cols(hh)] = y.astype(o_ref.dtype)


def _fox_attention(qkv, qa, ka, vt, g_fox, heads):
    t = qkv.shape[0]
    tq = _tile(t, 512)
    hp = 2 if heads % 2 == 0 else 1
    w = hp * HEAD_DIM
    groups = heads // hp
    per_group = lambda off: pl.BlockSpec((t, w), lambda h, i: (0, off + h))
    q_tile = pl.BlockSpec((tq, w), lambda h, i: (i, h))
    return pl.pallas_call(
        _fox_kernel,
        out_shape=jax.ShapeDtypeStruct((t, heads * HEAD_DIM), BF16),
        grid=(groups, t // tq),
        in_specs=[q_tile, q_tile, per_group(groups), per_group(0),
                  pl.BlockSpec((hp, V_ROWS, t), lambda h, i: (h, 0, 0)),
                  pl.BlockSpec((1, w), lambda h, i: (0, h))],
        out_specs=q_tile,
        scratch_shapes=[pltpu.VMEM((hp, tq, 2 * HEAD_DIM), BF16), pltpu.VMEM((hp, 1, tq), F32),
                        pltpu.VMEM((hp, V_ROWS, tq), F32)],
        compiler_params=_params("parallel", "arbitrary"),
        name="fox_attention",
    )(qkv, qa, qkv, ka, vt, g_fox)


def _hgrn_sum_matrix(c):
    levels = int(math.log2(c))
    p = np.arange(c)[:, None]
    j = np.arange(c)[None, :]
    blocks = []
    for lv in range(levels):
        h = c >> (lv + 1)
        second = ((p // h) % 2) == 1
        m_second = (p // h) * h
        m_first = (p // h + 1) * h
        blocks.append(np.where(second, (j >= m_second) & (j <= p), (j > p) & (j < m_first)))
    blocks.append(j <= p)
    blocks.append(j > p)
    return np.concatenate(blocks, axis=0).astype(np.float32)


def _hgrn_kernel(q_ref, f_ref, i_ref, gate_ref, lbl_ref, g_ref, nmat_ref, o_ref, st_ref, *, chunk, layer):
    tt = q_ref.shape[0]
    levels = int(math.log2(chunk))
    pair = 2 * chunk
    nt = lambda a, b: lax.dot_general(a, b, (((1,), (1,)), ((), ())), preferred_element_type=F32)

    @pl.when(pl.program_id(1) == 0)
    def _():
        st_ref[...] = jnp.zeros_like(st_ref)

    lbl = lbl_ref[...]
    e = jnp.exp(lbl - jnp.max(lbl, axis=0, keepdims=True))
    lb = jnp.sum(e[0:layer + 1, :], axis=0, keepdims=True) / jnp.sum(e, axis=0, keepdims=True)

    rowp = lax.broadcasted_iota(jnp.int32, (pair, HEAD_DIM), 0)
    tpos = lax.broadcasted_iota(jnp.int32, (pair, pair), 0)
    spos = lax.broadcasted_iota(jnp.int32, (pair, pair), 1)
    xor = jnp.bitwise_xor(tpos, spos)
    level_of = jnp.where((spos < tpos) & (xor < chunk), 0, -1)
    for lv in range(1, levels):
        level_of = jnp.where((spos < tpos) & (xor < (chunk >> lv)), lv, level_of)
    nmat = nmat_ref[...]

    fg = lb + (1.0 - lb) * jax.nn.sigmoid(f_ref[...])
    logf = jnp.log(fg)
    kk = 1.0 - fg
    qx = q_ref[...]
    qf = qx * jax.nn.sigmoid(qx)
    iv = i_ref[...]
    hi = logf.astype(BF16)
    lo = (logf - hi.astype(F32)).astype(BF16)

    dec = []
    for r0 in range(0, tt, pair):
        r1, r2 = r0 + chunk, r0 + pair
        rhs = jnp.concatenate([jnp.concatenate([hi[r0:r1], lo[r0:r1]], axis=0),
                               jnp.concatenate([hi[r1:r2], lo[r1:r2]], axis=0)], axis=1)
        dec.append(jnp.exp(jnp.dot(nmat, rhs, preferred_element_type=F32)))

    def block(p, blk):
        d = dec[p][blk * chunk:(blk + 1) * chunk]
        return jnp.concatenate([d[:, :HEAD_DIM], d[:, HEAD_DIM:]], axis=0)

    npairs = tt // pair
    rows = [slice(p * pair, (p + 1) * pair) for p in range(npairs)]
    a = [jnp.where(xor == 0, nt(qf[r].astype(BF16), kk[r].astype(BF16)), 0.0) for r in rows]
    for lv in range(levels):
        second = jnp.bitwise_and(rowp, chunk >> (lv + 1)) != 0
        for p, r in enumerate(rows):
            xl = (jnp.where(second, qf[r], kk[r]) * block(p, lv)).astype(BF16)
            a[p] = jnp.where(level_of == lv, nt(xl, xl), a[p])
    o_intra = [jnp.dot(a[p].astype(BF16), iv[r], preferred_element_type=F32) for p, r in enumerate(rows)]
    d_inc = jnp.concatenate([block(p, levels) for p in range(npairs)], axis=0)
    d_suf = jnp.concatenate([block(p, levels + 1) for p in range(npairs)], axis=0)
    qd = (qf * d_inc).astype(BF16)
    kd = (kk * d_suf).astype(BF16)
    upd = [lax.dot_general(iv[c:c + chunk], kd[c:c + chunk], (((0,), (0,)), ((), ())),
                           preferred_element_type=F32) for c in range(0, tt, chunk)]

    st = st_ref[...]
    o_inter = []
    for n, c in enumerate(range(0, tt, chunk)):
        o_inter.append(nt(qd[c:c + chunk], st.astype(BF16)))
        st = d_inc[c + chunk - 1:c + chunk, :] * st + upd[n]
    st_ref[...] = st

    o = jnp.concatenate(o_intra, axis=0) + jnp.concatenate(o_inter, axis=0)
    y = o * lax.rsqrt(jnp.mean(o * o, axis=-1, keepdims=True) + EPS) * g_ref[...]
    gx = gate_ref[...]
    o_ref[...] = (y * (gx * jax.nn.sigmoid(gx))).astype(o_ref.dtype)


def _hgrn2(qf_proj, i_proj, g_proj, lb_logits, g_out, heads, layer):
    t = i_proj.shape[0]
    chunk = min(HGRN_CHUNK, t)
    tt = _tile(t, 16 * chunk)
    sums = _hgrn_sum_matrix(chunk)
    nmat = jnp.asarray(np.concatenate([sums, sums], axis=1), dtype=BF16)
    nl = lb_logits.shape[0]
    blk = lambda off: pl.BlockSpec((tt, HEAD_DIM), lambda h, i: (i, off + h))
    return pl.pallas_call(
        functools.partial(_hgrn_kernel, chunk=chunk, layer=layer),
        out_shape=jax.ShapeDtypeStruct((t, heads * HEAD_DIM), BF16),
        grid=(heads, t // tt),
        in_specs=[blk(0), blk(heads), blk(0), blk(0),
                  pl.BlockSpec((nl, HEAD_DIM), lambda h, i: (0, h)),
                  pl.BlockSpec((1, HEAD_DIM), lambda h, i: (0, h)),
                  pl.BlockSpec(nmat.shape, lambda h, i: (0, 0))],
        out_specs=blk(0),
        scratch_shapes=[pltpu.VMEM((HEAD_DIM, HEAD_DIM), F32)],
        compiler_params=_params("parallel", "arbitrary"),
        name="hgrn2",
    )(qf_proj, qf_proj, i_proj, g_proj, lb_logits, g_out, nmat)


def _ffn_up_kernel(h_ref, wa_ref, wv_ref, cwa_ref, cwv_ref, cba_ref, cbv_ref, o_ref,
                   wab_ref, wvb_ref, ua_ref, uv_ref):
    tm = h_ref.shape[0]
    halo = SUBLANES

    @pl.when(pl.program_id(1) == 0)
    def _():
        wab_ref[...] = wa_ref[...].astype(BF16)
        wvb_ref[...] = wv_ref[...].astype(BF16)
        ua_ref[0:halo, :] = jnp.zeros((halo, ua_ref.shape[1]), F32)
        uv_ref[0:halo, :] = jnp.zeros((halo, uv_ref.shape[1]), F32)

    def conv(w_ref, cw_ref, cb_ref, u_ref):
        u = jnp.dot(h_ref[...], w_ref[...], preferred_element_type=F32)
        u_ref[halo:halo + tm, :] = u
        y = cb_ref[...] + cw_ref[0:1, :] * u_ref[halo - 2:halo - 2 + tm, :]
        y = y + cw_ref[1:2, :] * u_ref[halo - 1:halo - 1 + tm, :]
        y = y + cw_ref[2:3, :] * u
        u_ref[0:halo, :] = u_ref[tm:tm + halo, :]
        return y

    ya = conv(wab_ref, cwa_ref, cba_ref, ua_ref)
    yv = conv(wvb_ref, cwv_ref, cbv_ref, uv_ref)
    o_ref[...] = (ya * jax.nn.sigmoid(ya) * yv).astype(o_ref.dtype)


def _ffn_up(h, w_up, conv_w, conv_b):
    t, d = h.shape
    dff = w_up.shape[1] // 2
    tm = _tile(t, 1024)
    tn = _tile(dff, 256)
    nj = dff // tn
    return pl.pallas_call(
        _ffn_up_kernel,
        out_shape=jax.ShapeDtypeStruct((t, dff), BF16),
        grid=(nj, t // tm),
        in_specs=[pl.BlockSpec((tm, d), lambda j, i: (i, 0)),
                  pl.BlockSpec((d, tn), lambda j, i: (0, j)),
                  pl.BlockSpec((d, tn), lambda j, i: (0, nj + j)),
                  pl.BlockSpec((CONV_WIDTH, tn), lambda j, i: (0, j)),
                  pl.BlockSpec((CONV_WIDTH, tn), lambda j, i: (0, nj + j)),
                  pl.BlockSpec((1, tn), lambda j, i: (0, j)),
                  pl.BlockSpec((1, tn), lambda j, i: (0, nj + j))],
        out_specs=pl.BlockSpec((tm, tn), lambda j, i: (i, j)),
        scratch_shapes=[pltpu.VMEM((d, tn), BF16), pltpu.VMEM((d, tn), BF16),
                        pltpu.VMEM((tm + SUBLANES, tn), F32), pltpu.VMEM((tm + SUBLANES, tn), F32)],
        compiler_params=_params("parallel", "arbitrary"),
        name="ffn_up_conv_glu",
    )(h, w_up, w_up, conv_w, conv_w, conv_b, conv_b)


def _ffn_down_kernel(a_ref, w_ref, x_ref, g_ref, o_ref):
    acc = jnp.dot(a_ref[...], w_ref[...], preferred_element_type=F32)
    o_ref[...] = x_ref[...] + g_ref[...] * acc


def _ffn_down(act, w_down_bf16, x, gate):
    t, k = act.shape
    n = w_down_bf16.shape[1]
    tm = _tile(t, 512)
    tn = _tile(n, 256)
    return pl.pallas_call(
        _ffn_down_kernel,
        out_shape=jax.ShapeDtypeStruct((t, n), F32),
        grid=(t // tm, n // tn),
        in_specs=[pl.BlockSpec((tm, k), lambda i, j: (i, 0)),
                  pl.BlockSpec((k, tn), lambda i, j: (0, j)),
                  pl.BlockSpec((tm, tn), lambda i, j: (i, j)),
                  pl.BlockSpec((1, tn), lambda i, j: (0, j))],
        out_specs=pl.BlockSpec((tm, tn), lambda i, j: (i, j)),
        compiler_params=_params("parallel", "parallel"),
        name="ffn_down",
    )(act, w_down_bf16, x, gate)


def kernel(x, c, w_ada, b_ada, g_mix_norm, w_in, b_fox_f, hgrn_lb_logits, g_fox_out, g_hgrn_out,
           w_out, g_ffn_norm, w_up, conv_w, conv_b, w_down, g_final):
    b, t, d = x.shape
    assert b == 1, "single-sequence layer"
    depth = w_ada.shape[0]
    fox_heads = b_fox_f.shape[1]
    fox_width = g_fox_out.shape[1]
    hg_kwidth = hgrn_lb_logits.shape[1]
    hg_vwidth = g_hgrn_out.shape[1]
    hg_heads = hg_vwidth // HEAD_DIM
    assert fox_width == fox_heads * HEAD_DIM and hg_kwidth == hg_heads * HEAD_DIM
    assert fox_heads <= LANES
    fox_f0 = 3 * fox_width
    hg_q0 = fox_f0 + fox_heads

    xs = x.reshape(t, d)
    row = lambda v: v.reshape(1, -1)
    for l in range(depth):
        mod = _ada_mod(c, w_ada[l], b_ada[l])
        sh1, sc1, gt1, sh2, sc2, gt2 = [mod[:, k * d:(k + 1) * d] for k in range(N_MOD)]

        h1 = _norm_mod(xs, row(g_mix_norm[l]), sc1, sh1, BF16)
        w_t = jnp.swapaxes(w_in[l], 0, 1)
        b_ff = jnp.pad(b_fox_f[l], (0, LANES - fox_heads)).reshape(1, LANES)
        in_proj = functools.partial(_proj, [h1], w_t, transposed=True)

        qkv = in_proj(0, 3 * fox_width, BF16, lead_cols=fox_width,
                      lead_scale=LOG2E * HEAD_DIM ** -0.5, name="proj_fox_qkv")
        ff = in_proj(fox_f0, LANES, F32, name="proj_fox_f")
        hqf = in_proj(hg_q0, 2 * hg_kwidth, F32, name="proj_hgrn_qf")
        hi = in_proj(hg_q0 + 2 * hg_kwidth, hg_vwidth, BF16, name="proj_hgrn_i")
        hg = in_proj(hg_q0 + 2 * hg_kwidth + hg_vwidth, hg_vwidth, F32, name="proj_hgrn_g")

        qa, ka = _fox_bias_terms(ff, b_ff, fox_heads)
        vt = qkv[:, 2 * fox_width:].reshape(t, fox_heads, HEAD_DIM).transpose(1, 2, 0)
        vt = jnp.concatenate([vt, jnp.ones((fox_heads, V_ROWS - HEAD_DIM, t), BF16)], axis=1)
        o_fox = _fox_attention(qkv, qa, ka, vt, row(g_fox_out[l]), fox_heads)
        o_hg = _hgrn2(hqf, hi, hg, hgrn_lb_logits, row(g_hgrn_out[l]), hg_heads, l)

        xs = _proj([o_fox, o_hg], w_out[l], 0, d, F32, res=xs, gate=gt1, name="proj_out")

        h2 = _norm_mod(xs, row(g_ffn_norm[l]), sc2, sh2, BF16)
        act = _ffn_up(h2, w_up[l], conv_w[l], row(conv_b[l]))
        xs = _ffn_down(act, w_down[l].astype(BF16), xs, gt2)

    out = _norm_mod(xs, row(g_final), None, None, x.dtype)
    return out.reshape(b, t, d)
```

```python
import functools
import math

import numpy as np
import jax
import jax.numpy as jnp
from jax import lax
from jax.experimental import pallas as pl
from jax.experimental.pallas import tpu as pltpu

F32 = jnp.float32
BF16 = jnp.bfloat16

EPS = 1e-6
CONV_WIDTH = 3
N_MOD = 6
HEAD_DIM = 128
HGRN_CHUNK = 64
LANES = 128
SUBLANES = 8
VMEM_LIMIT = 56 * 1024 * 1024
LOG2E = 1.4426950408889634


def _params(*sem):
    return pltpu.CompilerParams(dimension_semantics=sem, vmem_limit_bytes=VMEM_LIMIT)


def _tile(n, pref):
    if n <= pref:
        return n
    t = pref
    while n % t:
        t //= 2
    return t


def _ada_kernel(cb_ref, w_ref, b_ref, o_ref, cond_ref):
    @pl.when(pl.program_id(0) == 0)
    def _():
        cb = cb_ref[...]
        cond_ref[...] = cb * jax.nn.sigmoid(cb)

    tn = o_ref.shape[1]
    for j in range(tn // LANES):
        sl = slice(j * LANES, (j + 1) * LANES)
        col = jnp.sum(w_ref[:, sl] * cond_ref[...], axis=0, keepdims=True)
        o_ref[:, sl] = col + b_ref[:, sl]


def _ada_mod(c, w_ada, b_ada):
    d, n = w_ada.shape
    cb = jnp.broadcast_to(c.reshape(d, 1), (d, LANES))
    tn = _tile(n, 512)
    return pl.pallas_call(
        _ada_kernel,
        out_shape=jax.ShapeDtypeStruct((1, n), F32),
        grid=(n // tn,),
        in_specs=[pl.BlockSpec((d, LANES), lambda j: (0, 0)),
                  pl.BlockSpec((d, tn), lambda j: (0, j)),
                  pl.BlockSpec((1, tn), lambda j: (0, j))],
        out_specs=pl.BlockSpec((1, tn), lambda j: (0, j)),
        scratch_shapes=[pltpu.VMEM((d, LANES), F32)],
        compiler_params=_params("arbitrary"),
        name="ada_mod",
    )(cb, w_ada, b_ada.reshape(1, n))


def _norm_mod_kernel(x_ref, g_ref, sc_ref, sh_ref, o_ref):
    x = x_ref[...]
    y = x * lax.rsqrt(jnp.mean(x * x, axis=-1, keepdims=True) + EPS) * g_ref[...]
    o_ref[...] = (y * (1.0 + sc_ref[...]) + sh_ref[...]).astype(o_ref.dtype)


def _norm_kernel(x_ref, g_ref, o_ref):
    x = x_ref[...]
    y = x * lax.rsqrt(jnp.mean(x * x, axis=-1, keepdims=True) + EPS) * g_ref[...]
    o_ref[...] = y.astype(o_ref.dtype)


def _norm_mod(x, g, sc, sh, out_dtype):
    t, d = x.shape
    tm = _tile(t, 256)
    row = pl.BlockSpec((tm, d), lambda i: (i, 0))
    vec = pl.BlockSpec((1, d), lambda i: (0, 0))
    if sc is None:
        kern, args, specs = _norm_kernel, (x, g), [row, vec]
    else:
        kern, args, specs = _norm_mod_kernel, (x, g, sc, sh), [row, vec, vec, vec]
    return pl.pallas_call(
        kern,
        out_shape=jax.ShapeDtypeStruct((t, d), out_dtype),
        grid=(t // tm,),
        in_specs=specs,
        out_specs=row,
        compiler_params=_params("parallel"),
        name="rms_norm",
    )(*args)


def _proj_kernel(*refs, n_a, has_res, lead_tiles, lead_scale, transposed):
    a_refs = refs[:n_a]
    w_ref = refs[n_a]
    pos = n_a + 1
    if has_res:
        x_ref, g_ref = refs[pos], refs[pos + 1]
        pos += 2
    o_ref, wb_ref = refs[pos], refs[pos + 1]

    @pl.when(pl.program_id(1) == 0)
    def _():
        wb_ref[...] = w_ref[...].astype(BF16)

    acc = None
    k0 = 0
    for a_ref in a_refs:
        ks = a_ref.shape[1]
        if transposed:
            part = lax.dot_general(a_ref[...], wb_ref[:, k0:k0 + ks], (((1,), (1,)), ((), ())),
                                   preferred_element_type=F32)
        else:
            part = jnp.dot(a_ref[...], wb_ref[k0:k0 + ks, :], preferred_element_type=F32)
        acc = part if acc is None else acc + part
        k0 += ks
    if lead_tiles:
        acc = acc * jnp.where(pl.program_id(0) < lead_tiles, lead_scale, 1.0).astype(F32)
    if has_res:
        acc = x_ref[...] + g_ref[...] * acc
    o_ref[...] = acc.astype(o_ref.dtype)


def _proj(a_list, w, col0, n, out_dtype, res=None, gate=None, lead_cols=0, lead_scale=1.0,
          transposed=False, name="proj"):
    m = a_list[0].shape[0]
    k = w.shape[1] if transposed else w.shape[0]
    assert sum(a.shape[1] for a in a_list) == k
    tm = _tile(m, 1024)
    tn = _tile(n, 512)
    assert lead_cols % tn == 0
    in_specs = [pl.BlockSpec((tm, a.shape[1]), lambda j, i: (i, 0)) for a in a_list]
    if transposed:
        assert col0 % SUBLANES == 0
        in_specs.append(pl.BlockSpec((pl.Element(tn), pl.Element(k)),
                                     lambda j, i: (pl.multiple_of(col0 + j * tn, SUBLANES), 0)))
        wb_shape = (tn, k)
    else:
        assert col0 % tn == 0
        in_specs.append(pl.BlockSpec((k, tn), lambda j, i: (0, col0 // tn + j)))
        wb_shape = (k, tn)
    args = list(a_list) + [w]
    if res is not None:
        in_specs += [pl.BlockSpec((tm, tn), lambda j, i: (i, j)),
                     pl.BlockSpec((1, tn), lambda j, i: (0, j))]
        args += [res, gate]
    return pl.pallas_call(
        functools.partial(_proj_kernel, n_a=len(a_list), has_res=res is not None,
                          lead_tiles=lead_cols // tn, lead_scale=lead_scale, transposed=transposed),
        out_shape=jax.ShapeDtypeStruct((m, n), out_dtype),
        grid=(n // tn, m // tm),
        in_specs=in_specs,
        out_specs=pl.BlockSpec((tm, tn), lambda j, i: (i, j)),
        scratch_shapes=[pltpu.VMEM(wb_shape, BF16)],
        compiler_params=_params("parallel", "arbitrary"),
        name=name,
    )(*args)


def _split3(x):
    hi = x.astype(BF16)
    r1 = x - hi.astype(F32)
    mid = r1.astype(BF16)
    lo = (r1 - mid.astype(F32)).astype(BF16)
    return hi, mid, lo


def _cum_kernel(ff_ref, b_ref, qa_ref, ka_ref, carry_ref, *, heads):
    @pl.when(pl.program_id(0) == 0)
    def _():
        carry_ref[...] = jnp.zeros_like(carry_ref)

    z = ff_ref[...] + b_ref[...]
    logf = jnp.minimum(z, 0.0) - jnp.log1p(jnp.exp(-jnp.abs(z)))
    tt = z.shape[0]
    row = lax.broadcasted_iota(jnp.int32, (tt, tt), 0)
    col = lax.broadcasted_iota(jnp.int32, (tt, tt), 1)
    tril = jnp.where(col <= row, 1.0, 0.0).astype(BF16)
    local = None
    for piece in _split3(logf):
        part = jnp.dot(tril, piece, preferred_element_type=F32)
        local = part if local is None else local + part
    cum = local + carry_ref[0:1, :]
    carry_ref[...] = jnp.broadcast_to(cum[tt - 1:tt, :], carry_ref.shape)

    c2 = cum * LOG2E
    lane = lax.broadcasted_iota(jnp.int32, (tt, LANES), 1)
    for h in range(heads):
        hi, mid, lo = [p.astype(F32) for p in _split3(c2[:, h:h + 1])]
        qa = jnp.where(lane == 0, hi, jnp.where(lane == 1, mid, jnp.where(lane == 2, lo,
                       jnp.where(lane < 6, 1.0, 0.0))))
        ka = jnp.where(lane < 3, 1.0, jnp.where(lane == 3, -hi, jnp.where(lane == 4, -mid,
                       jnp.where(lane == 5, -lo, 0.0))))
        qa_ref[:, h * LANES:(h + 1) * LANES] = qa.astype(BF16)
        ka_ref[:, h * LANES:(h + 1) * LANES] = ka.astype(BF16)


def _fox_bias_terms(ff, b_pad, heads):
    t, n = ff.shape
    tt = _tile(t, 256)
    out = jax.ShapeDtypeStruct((t, heads * LANES), BF16)
    return pl.pallas_call(
        functools.partial(_cum_kernel, heads=heads),
        out_shape=(out, out),
        grid=(t // tt,),
        in_specs=[pl.BlockSpec((tt, n), lambda i: (i, 0)),
                  pl.BlockSpec((1, n), lambda i: (0, 0))],
        out_specs=(pl.BlockSpec((tt, heads * LANES), lambda i: (i, 0)),
                   pl.BlockSpec((tt, heads * LANES), lambda i: (i, 0))),
        scratch_shapes=[pltpu.VMEM((SUBLANES, n), F32)],
        compiler_params=_params("arbitrary"),
        name="fox_bias_terms",
    )(ff, b_pad)


V_ROWS = HEAD_DIM + 16


def _fox_kernel(q_ref, qa_ref, k_ref, ka_ref, vt_ref, g_ref, o_ref, qc_ref, m_ref, acc_ref):
    i = pl.program_id(1)
    tq = q_ref.shape[0]
    tk = tq
    hp = vt_ref.shape[0]
    cols = lambda hh: slice(hh * HEAD_DIM, (hh + 1) * HEAD_DIM)

    for hh in range(hp):
        qc_ref[hh, :, 0:HEAD_DIM] = q_ref[:, cols(hh)]
        qc_ref[hh, :, HEAD_DIM:2 * HEAD_DIM] = qa_ref[:, cols(hh)]
    m_ref[...] = jnp.full_like(m_ref, -jnp.inf)
    acc_ref[...] = jnp.zeros_like(acc_ref)

    def tile(hh, j, masked):
        k0 = pl.multiple_of(j * tk, tk)
        kc = jnp.concatenate([k_ref[pl.ds(k0, tk), cols(hh)], ka_ref[pl.ds(k0, tk), cols(hh)]], axis=1)
        s = lax.dot_general(kc, qc_ref[hh], (((1,), (1,)), ((), ())),
                            preferred_element_type=F32)
        if masked:
            key = lax.broadcasted_iota(jnp.int32, (tk, tq), 0)
            qry = lax.broadcasted_iota(jnp.int32, (tk, tq), 1)
            s = jnp.where(key <= qry, s, -jnp.inf)
        m_old = m_ref[hh]
        m_new = jnp.maximum(m_old, jnp.max(s, axis=0, keepdims=True))
        alpha = jnp.exp2(m_old - m_new)
        p = jnp.exp2(s - m_new).astype(BF16)
        pv = jnp.dot(vt_ref[hh, :, pl.ds(k0, tk)], p, preferred_element_type=F32)
        acc_ref[hh] = alpha * acc_ref[hh] + pv
        m_ref[hh] = m_new

    def body(j, carry):
        for hh in range(hp):
            tile(hh, j, False)
        return carry

    lax.fori_loop(0, i, body, 0)
    for hh in range(hp):
        tile(hh, i, True)

    for hh in range(hp):
        acc = acc_ref[hh]
        o = (acc[0:HEAD_DIM, :] / acc[HEAD_DIM:HEAD_DIM + 1, :]).T
        y = o * lax.rsqrt(jnp.mean(o * o, axis=-1, keepdims=True) + EPS) * g_ref[:, cols(hh)]
        o_ref[:, cols(hh)] = y.astype(o_ref.dtype)


def _fox_attention(qkv, qa, ka, vt, g_fox, heads):
    t = qkv.shape[0]
    tq = _tile(t, 512)
    hp = 2 if heads % 2 == 0 else 1
    w = hp * HEAD_DIM
    groups = heads // hp
    per_group = lambda off: pl.BlockSpec((t, w), lambda h, i: (0, off + h))
    q_tile = pl.BlockSpec((tq, w), lambda h, i: (i, h))
    return pl.pallas_call(
        _fox_kernel,
        out_shape=jax.ShapeDtypeStruct((t, heads * HEAD_DIM), BF16),
        grid=(groups, t // tq),
        in_specs=[q_tile, q_tile, per_group(groups), per_group(0),
                  pl.BlockSpec((hp, V_ROWS, t), lambda h, i: (h, 0, 0)),
                  pl.BlockSpec((1, w), lambda h, i: (0, h))],
        out_specs=q_tile,
        scratch_shapes=[pltpu.VMEM((hp, tq, 2 * HEAD_DIM), BF16), pltpu.VMEM((hp, 1, tq), F32),
                        pltpu.VMEM((hp, V_ROWS, tq), F32)],
        compiler_params=_params("parallel", "arbitrary"),
        name="fox_attention",
    )(qkv, qa, qkv, ka, vt, g_fox)


def _hgrn_sum_matrix(c):
    levels = int(math.log2(c))
    p = np.arange(c)[:, None]
    j = np.arange(c)[None, :]
    blocks = []
    for lv in range(levels):
        h = c >> (lv + 1)
        second = ((p // h) % 2) == 1
        m_second = (p // h) * h
        m_first = (p // h + 1) * h
        blocks.append(np.where(second, (j >= m_second) & (j <= p), (j > p) & (j < m_first)))
    blocks.append(j <= p)
    blocks.append(j > p)
    return np.concatenate(blocks, axis=0).astype(np.float32)


def _hgrn_kernel(q_ref, f_ref, i_ref, gate_ref, lbl_ref, g_ref, nmat_ref, o_ref, st_ref, *, chunk, layer):
    tt = q_ref.shape[0]
    levels = int(math.log2(chunk))
    pair = 2 * chunk
    nt = lambda a, b: lax.dot_general(a, b, (((1,), (1,)), ((), ())), preferred_element_type=F32)

    @pl.when(pl.program_id(1) == 0)
    def _():
        st_ref[...] = jnp.zeros_like(st_ref)

    lbl = lbl_ref[...]
    e = jnp.exp(lbl - jnp.max(lbl, axis=0, keepdims=True))
    lb = jnp.sum(e[0:layer + 1, :], axis=0, keepdims=True) / jnp.sum(e, axis=0, keepdims=True)

    rowp = lax.broadcasted_iota(jnp.int32, (pair, HEAD_DIM), 0)
    tpos = lax.broadcasted_iota(jnp.int32, (pair, pair), 0)
    spos = lax.broadcasted_iota(jnp.int32, (pair, pair), 1)
    xor = jnp.bitwise_xor(tpos, spos)
    level_of = jnp.where((spos < tpos) & (xor < chunk), 0, -1)
    for lv in range(1, levels):
        level_of = jnp.where((spos < tpos) & (xor < (chunk >> lv)), lv, level_of)
    nmat = nmat_ref[...]

    fg = lb + (1.0 - lb) * jax.nn.sigmoid(f_ref[...])
    logf = jnp.log(fg)
    kk = 1.0 - fg
    qx = q_ref[...]
    qf = qx * jax.nn.sigmoid(qx)
    iv = i_ref[...]
    hi = logf.astype(BF16)
    lo = (logf - hi.astype(F32)).astype(BF16)

    dec = []
    for r0 in range(0, tt, pair):
        r1, r2 = r0 + chunk, r0 + pair
        rhs = jnp.concatenate([jnp.concatenate([hi[r0:r1], lo[r0:r1]], axis=0),
                               jnp.concatenate([hi[r1:r2], lo[r1:r2]], axis=0)], axis=1)
        dec.append(jnp.exp(jnp.dot(nmat, rhs, preferred_element_type=F32)))

    def block(p, blk):
        d = dec[p][blk * chunk:(blk + 1) * chunk]
        return jnp.concatenate([d[:, :HEAD_DIM], d[:, HEAD_DIM:]], axis=0)

    npairs = tt // pair
    rows = [slice(p * pair, (p + 1) * pair) for p in range(npairs)]
    a = [jnp.where(xor == 0, nt(qf[r].astype(BF16), kk[r].astype(BF16)), 0.0) for r in rows]
    for lv in range(levels):
        second = jnp.bitwise_and(rowp, chunk >> (lv + 1)) != 0
        for p, r in enumerate(rows):
            xl = (jnp.where(second, qf[r], kk[r]) * block(p, lv)).astype(BF16)
            a[p] = jnp.where(level_of == lv, nt(xl, xl), a[p])
    o_intra = [jnp.dot(a[p].astype(BF16), iv[r], preferred_element_type=F32) for p, r in enumerate(rows)]
    d_inc = jnp.concatenate([block(p, levels) for p in range(npairs)], axis=0)
    d_suf = jnp.concatenate([block(p, levels + 1) for p in range(npairs)], axis=0)
    qd = (qf * d_inc).astype(BF16)
    kd = (kk * d_suf).astype(BF16)
    upd = [lax.dot_general(iv[c:c + chunk], kd[c:c + chunk], (((0,), (0,)), ((), ())),
                           preferred_element_type=F32) for c in range(0, tt, chunk)]

    st = st_ref[...]
    o_inter = []
    for n, c in enumerate(range(0, tt, chunk)):
        o_inter.append(nt(qd[c:c + chunk], st.astype(BF16)))
        st = d_inc[c + chunk - 1:c + chunk, :] * st + upd[n]
    st_ref[...] = st

    o = jnp.concatenate(o_intra, axis=0) + jnp.concatenate(o_inter, axis=0)
    y = o * lax.rsqrt(jnp.mean(o * o, axis=-1, keepdims=True) + EPS) * g_ref[...]
    gx = gate_ref[...]
    o_ref[...] = (y * (gx * jax.nn.sigmoid(gx))).astype(o_ref.dtype)


def _hgrn2(qf_proj, i_proj, g_proj, lb_logits, g_out, heads, layer):
    t = i_proj.shape[0]
    chunk = min(HGRN_CHUNK, t)
    tt = _tile(t, 16 * chunk)
    sums = _hgrn_sum_matrix(chunk)
    nmat = jnp.asarray(np.concatenate([sums, sums], axis=1), dtype=BF16)
    nl = lb_logits.shape[0]
    blk = lambda off: pl.BlockSpec((tt, HEAD_DIM), lambda h, i: (i, off + h))
    return pl.pallas_call(
        functools.partial(_hgrn_kernel, chunk=chunk, layer=layer),
        out_shape=jax.ShapeDtypeStruct((t, heads * HEAD_DIM), BF16),
        grid=(heads, t // tt),
        in_specs=[blk(0), blk(heads), blk(0), blk(0),
                  pl.BlockSpec((nl, HEAD_DIM), lambda h, i: (0, h)),
                  pl.BlockSpec((1, HEAD_DIM), lambda h, i: (0, h)),
                  pl.BlockSpec(nmat.shape, lambda h, i: (0, 0))],
        out_specs=blk(0),
        scratch_shapes=[pltpu.VMEM((HEAD_DIM, HEAD_DIM), F32)],
        compiler_params=_params("parallel", "arbitrary"),
        name="hgrn2",
    )(qf_proj, qf_proj, i_proj, g_proj, lb_logits, g_out, nmat)


MXU_DEPTH = 256


def _zero_from(x):
    bits = pltpu.bitcast(x, jnp.uint32)
    acc = None
    for r in range(0, x.shape[0], SUBLANES):
        for c in range(0, x.shape[1], LANES):
            blk = bits[r:r + SUBLANES, c:c + LANES]
            acc = blk if acc is None else acc | blk
    return (acc >> 16) >> 16


def _tie(slab, zero):
    pack = 2 * SUBLANES
    top = pltpu.bitcast(slab[0:pack, :], jnp.uint32) | jnp.tile(zero, (1, slab.shape[1] // LANES))
    return jnp.concatenate([pltpu.bitcast(top, slab.dtype), slab[pack:, :]], axis=0)


def _ffn_up_kernel(h_ref, wa_ref, wv_ref, cwa_ref, cwv_ref, cba_ref, cbv_ref, o_ref,
                   wab_ref, wvb_ref, ua_ref, uv_ref, *, n_row_tiles, n_steps):
    tm, d = h_ref.shape
    halo = SUBLANES
    slabs = d // MXU_DEPTH if d % MXU_DEPTH == 0 and tm % (d // MXU_DEPTH * SUBLANES) == 0 else 1
    sub = tm // slabs
    s = pl.program_id(0)
    row_tile = jnp.minimum(s, n_steps - 1) % n_row_tiles
    cur = s % 2
    prev = 1 - cur

    @pl.when(s == 0)
    def _():
        ua_ref[1] = jnp.zeros(ua_ref.shape[1:], F32)
        uv_ref[1] = jnp.zeros(uv_ref.shape[1:], F32)

    @pl.when(jnp.logical_and(row_tile == 0, s < n_steps))
    def _():
        wab_ref[...] = wa_ref[...].astype(BF16)
        wvb_ref[...] = wv_ref[...].astype(BF16)

    zeros = []
    for r in range(slabs):
        base = halo + r * sub

        def conv(cw_ref, cb_ref, u_ref):
            y = cb_ref[...] + cw_ref[0:1, :] * u_ref[prev, base - 2:base - 2 + sub, :]
            y = y + cw_ref[1:2, :] * u_ref[prev, base - 1:base - 1 + sub, :]
            return y + cw_ref[2:3, :] * u_ref[prev, base:base + sub, :]

        ya = conv(cwa_ref, cba_ref, ua_ref)
        yv = conv(cwv_ref, cbv_ref, uv_ref)
        act = ya * jax.nn.sigmoid(ya) * yv
        o_ref[r * sub:(r + 1) * sub, :] = act.astype(o_ref.dtype)
        zeros.append(_zero_from(act))

    first = row_tile == 0
    kw = d // slabs
    lhs = jnp.concatenate([_tie(h_ref[:, k * kw:(k + 1) * kw], zeros[k]) for k in range(slabs)], axis=1)
    for w_ref, u_ref in ((wab_ref, ua_ref), (wvb_ref, uv_ref)):
        u_ref[cur, 0:halo, :] = jnp.where(first, 0.0, u_ref[prev, tm:tm + halo, :])
        u_ref[cur, halo:halo + tm, :] = jnp.dot(lhs, w_ref[...], preferred_element_type=F32)


def _ffn_up(h, w_up, conv_w, conv_b):
    t, d = h.shape
    dff = w_up.shape[1] // 2
    tm = _tile(t, 1024)
    tn = _tile(dff, 256)
    nj = dff // tn
    ni = t // tm
    n_steps = nj * ni
    cur = lambda s: jnp.minimum(s, n_steps - 1)
    lag = lambda s: jnp.maximum(s - 1, 0)
    return pl.pallas_call(
        functools.partial(_ffn_up_kernel, n_row_tiles=ni, n_steps=n_steps),
        out_shape=jax.ShapeDtypeStruct((t, dff), BF16),
        grid=(n_steps + 1,),
        in_specs=[pl.BlockSpec((tm, d), lambda s: (cur(s) % ni, 0)),
                  pl.BlockSpec((d, tn), lambda s: (0, cur(s) // ni)),
                  pl.BlockSpec((d, tn), lambda s: (0, nj + cur(s) // ni)),
                  pl.BlockSpec((CONV_WIDTH, tn), lambda s: (0, lag(s) // ni)),
                  pl.BlockSpec((CONV_WIDTH, tn), lambda s: (0, nj + lag(s) // ni)),
                  pl.BlockSpec((1, tn), lambda s: (0, lag(s) // ni)),
                  pl.BlockSpec((1, tn), lambda s: (0, nj + lag(s) // ni))],
        out_specs=pl.BlockSpec((tm, tn), lambda s: (lag(s) % ni, lag(s) // ni)),
        scratch_shapes=[pltpu.VMEM((d, tn), BF16), pltpu.VMEM((d, tn), BF16),
                        pltpu.VMEM((2, tm + SUBLANES, tn), F32), pltpu.VMEM((2, tm + SUBLANES, tn), F32)],
        compiler_params=_params("arbitrary"),
        name="ffn_up_conv_glu",
    )(h, w_up, w_up, conv_w, conv_w, conv_b, conv_b)


def _ffn_down_kernel(a_ref, w_ref, x_ref, g_ref, o_ref):
    acc = jnp.dot(a_ref[...], w_ref[...], preferred_element_type=F32)
    o_ref[...] = x_ref[...] + g_ref[...] * acc


def _ffn_down(act, w_down_bf16, x, gate):
    t, k = act.shape
    n = w_down_bf16.shape[1]
    tm = _tile(t, 512)
    tn = _tile(n, 256)
    return pl.pallas_call(
        _ffn_down_kernel,
        out_shape=jax.ShapeDtypeStruct((t, n), F32),
        grid=(t // tm, n // tn),
        in_specs=[pl.BlockSpec((tm, k), lambda i, j: (i, 0)),
                  pl.BlockSpec((k, tn), lambda i, j: (0, j)),
                  pl.BlockSpec((tm, tn), lambda i, j: (i, j)),
                  pl.BlockSpec((1, tn), lambda i, j: (0, j))],
        out_specs=pl.BlockSpec((tm, tn), lambda i, j: (i, j)),
        compiler_params=_params("parallel", "parallel"),
        name="ffn_down",
    )(act, w_down_bf16, x, gate)


def kernel(x, c, w_ada, b_ada, g_mix_norm, w_in, b_fox_f, hgrn_lb_logits, g_fox_out, g_hgrn_out,
           w_out, g_ffn_norm, w_up, conv_w, conv_b, w_down, g_final):
    b, t, d = x.shape
    assert b == 1, "single-sequence layer"
    depth = w_ada.shape[0]
    fox_heads = b_fox_f.shape[1]
    fox_width = g_fox_out.shape[1]
    hg_kwidth = hgrn_lb_logits.shape[1]
    hg_vwidth = g_hgrn_out.shape[1]
    hg_heads = hg_vwidth // HEAD_DIM
    assert fox_width == fox_heads * HEAD_DIM and hg_kwidth == hg_heads * HEAD_DIM
    assert fox_heads <= LANES
    fox_f0 = 3 * fox_width
    hg_q0 = fox_f0 + fox_heads

    xs = x.reshape(t, d)
    row = lambda v: v.reshape(1, -1)
    for l in range(depth):
        mod = _ada_mod(c, w_ada[l], b_ada[l])
        sh1, sc1, gt1, sh2, sc2, gt2 = [mod[:, k * d:(k + 1) * d] for k in range(N_MOD)]

        h1 = _norm_mod(xs, row(g_mix_norm[l]), sc1, sh1, BF16)
        w_t = jnp.swapaxes(w_in[l], 0, 1)
        b_ff = jnp.pad(b_fox_f[l], (0, LANES - fox_heads)).reshape(1, LANES)
        in_proj = functools.partial(_proj, [h1], w_t, transposed=True)

        qkv = in_proj(0, 3 * fox_width, BF16, lead_cols=fox_width,
                      lead_scale=LOG2E * HEAD_DIM ** -0.5, name="proj_fox_qkv")
        ff = in_proj(fox_f0, LANES, F32, name="proj_fox_f")
        hqf = in_proj(hg_q0, 2 * hg_kwidth, F32, name="proj_hgrn_qf")
        hi = in_proj(hg_q0 + 2 * hg_kwidth, hg_vwidth, BF16, name="proj_hgrn_i")
        hg = in_proj(hg_q0 + 2 * hg_kwidth + hg_vwidth, hg_vwidth, F32, name="proj_hgrn_g")

        qa, ka = _fox_bias_terms(ff, b_ff, fox_heads)
        vt = qkv[:, 2 * fox_width:].reshape(t, fox_heads, HEAD_DIM).transpose(1, 2, 0)
        vt = jnp.concatenate([vt, jnp.ones((fox_heads, V_ROWS - HEAD_DIM, t), BF16)], axis=1)
        o_fox = _fox_attention(qkv, qa, ka, vt, row(g_fox_out[l]), fox_heads)
        o_hg = _hgrn2(hqf, hi, hg, hgrn_lb_logits, row(g_hgrn_out[l]), hg_heads, l)

        xs = _proj([o_fox, o_hg], w_out[l], 0, d, F32, res=xs, gate=gt1, name="proj_out")

        h2 = _norm_mod(xs, row(g_ffn_norm[l]), sc2, sh2, BF16)
        act = _ffn_up(h2, w_up[l], conv_w[l], row(conv_b[l]))
        xs = _ffn_down(act, w_down[l].astype(BF16), xs, gt2)

    out = _norm_mod(xs, row(g_final), None, None, x.dtype)
    return out.reshape(b, t, d)
```

```python
import functools
import math

import numpy as np
import jax
import jax.numpy as jnp
from jax import lax
from jax.experimental import pallas as pl
from jax.experimental.pallas import tpu as pltpu

F32 = jnp.float32
BF16 = jnp.bfloat16

EPS = 1e-6
CONV_WIDTH = 3
N_MOD = 6
HEAD_DIM = 128
HGRN_CHUNK = 64
LANES = 128
SUBLANES = 8
VMEM_LIMIT = 56 * 1024 * 1024
LOG2E = 1.4426950408889634


def _params(*sem):
    return pltpu.CompilerParams(dimension_semantics=sem, vmem_limit_bytes=VMEM_LIMIT)


def _tile(n, pref):
    if n <= pref:
        return n
    t = pref
    while n % t:
        t //= 2
    return t


def _ada_kernel(cb_ref, w_ref, b_ref, o_ref, cond_ref):
    @pl.when(pl.program_id(0) == 0)
    def _():
        cb = cb_ref[...]
        cond_ref[...] = cb * jax.nn.sigmoid(cb)

    tn = o_ref.shape[1]
    for j in range(tn // LANES):
        sl = slice(j * LANES, (j + 1) * LANES)
        col = jnp.sum(w_ref[:, sl] * cond_ref[...], axis=0, keepdims=True)
        o_ref[:, sl] = col + b_ref[:, sl]


def _ada_mod(c, w_ada, b_ada):
    d, n = w_ada.shape
    cb = jnp.broadcast_to(c.reshape(d, 1), (d, LANES))
    tn = _tile(n, 512)
    return pl.pallas_call(
        _ada_kernel,
        out_shape=jax.ShapeDtypeStruct((1, n), F32),
        grid=(n // tn,),
        in_specs=[pl.BlockSpec((d, LANES), lambda j: (0, 0)),
                  pl.BlockSpec((d, tn), lambda j: (0, j)),
                  pl.BlockSpec((1, tn), lambda j: (0, j))],
        out_specs=pl.BlockSpec((1, tn), lambda j: (0, j)),
        scratch_shapes=[pltpu.VMEM((d, LANES), F32)],
        compiler_params=_params("arbitrary"),
        name="ada_mod",
    )(cb, w_ada, b_ada.reshape(1, n))


def _norm_mod_kernel(x_ref, g_ref, sc_ref, sh_ref, o_ref):
    x = x_ref[...]
    y = x * lax.rsqrt(jnp.mean(x * x, axis=-1, keepdims=True) + EPS) * g_ref[...]
    o_ref[...] = (y * (1.0 + sc_ref[...]) + sh_ref[...]).astype(o_ref.dtype)


def _norm_kernel(x_ref, g_ref, o_ref):
    x = x_ref[...]
    y = x * lax.rsqrt(jnp.mean(x * x, axis=-1, keepdims=True) + EPS) * g_ref[...]
    o_ref[...] = y.astype(o_ref.dtype)


def _norm_mod(x, g, sc, sh, out_dtype):
    t, d = x.shape
    tm = _tile(t, 256)
    row = pl.BlockSpec((tm, d), lambda i: (i, 0))
    vec = pl.BlockSpec((1, d), lambda i: (0, 0))
    if sc is None:
        kern, args, specs = _norm_kernel, (x, g), [row, vec]
    else:
        kern, args, specs = _norm_mod_kernel, (x, g, sc, sh), [row, vec, vec, vec]
    return pl.pallas_call(
        kern,
        out_shape=jax.ShapeDtypeStruct((t, d), out_dtype),
        grid=(t // tm,),
        in_specs=specs,
        out_specs=row,
        compiler_params=_params("parallel"),
        name="rms_norm",
    )(*args)


def _proj_kernel(*refs, n_a, has_res, lead_tiles, lead_scale, transposed):
    a_refs = refs[:n_a]
    w_ref = refs[n_a]
    pos = n_a + 1
    if has_res:
        x_ref, g_ref = refs[pos], refs[pos + 1]
        pos += 2
    o_ref, wb_ref = refs[pos], refs[pos + 1]

    @pl.when(pl.program_id(1) == 0)
    def _():
        wb_ref[...] = w_ref[...].astype(BF16)

    acc = None
    k0 = 0
    for a_ref in a_refs:
        ks = a_ref.shape[1]
        if transposed:
            part = lax.dot_general(a_ref[...], wb_ref[:, k0:k0 + ks], (((1,), (1,)), ((), ())),
                                   preferred_element_type=F32)
        else:
            part = jnp.dot(a_ref[...], wb_ref[k0:k0 + ks, :], preferred_element_type=F32)
        acc = part if acc is None else acc + part
        k0 += ks
    if lead_tiles:
        acc = acc * jnp.where(pl.program_id(0) < lead_tiles, lead_scale, 1.0).astype(F32)
    if has_res:
        acc = x_ref[...] + g_ref[...] * acc
    o_ref[...] = acc.astype(o_ref.dtype)


def _proj(a_list, w, col0, n, out_dtype, res=None, gate=None, lead_cols=0, lead_scale=1.0,
          transposed=False, name="proj"):
    m = a_list[0].shape[0]
    k = w.shape[1] if transposed else w.shape[0]
    assert sum(a.shape[1] for a in a_list) == k
    tm = _tile(m, 1024)
    tn = _tile(n, 512)
    assert lead_cols % tn == 0
    in_specs = [pl.BlockSpec((tm, a.shape[1]), lambda j, i: (i, 0)) for a in a_list]
    if transposed:
        assert col0 % SUBLANES == 0
        in_specs.append(pl.BlockSpec((pl.Element(tn), pl.Element(k)),
                                     lambda j, i: (pl.multiple_of(col0 + j * tn, SUBLANES), 0)))
        wb_shape = (tn, k)
    else:
        assert col0 % tn == 0
        in_specs.append(pl.BlockSpec((k, tn), lambda j, i: (0, col0 // tn + j)))
        wb_shape = (k, tn)
    args = list(a_list) + [w]
    if res is not None:
        in_specs += [pl.BlockSpec((tm, tn), lambda j, i: (i, j)),
                     pl.BlockSpec((1, tn), lambda j, i: (0, j))]
        args += [res, gate]
    return pl.pallas_call(
        functools.partial(_proj_kernel, n_a=len(a_list), has_res=res is not None,
                          lead_tiles=lead_cols // tn, lead_scale=lead_scale, transposed=transposed),
        out_shape=jax.ShapeDtypeStruct((m, n), out_dtype),
        grid=(n // tn, m // tm),
        in_specs=in_specs,
        out_specs=pl.BlockSpec((tm, tn), lambda j, i: (i, j)),
        scratch_shapes=[pltpu.VMEM(wb_shape, BF16)],
        compiler_params=_params("parallel", "arbitrary"),
        name=name,
    )(*args)


def _split3(x):
    hi = x.astype(BF16)
    r1 = x - hi.astype(F32)
    mid = r1.astype(BF16)
    lo = (r1 - mid.astype(F32)).astype(BF16)
    return hi, mid, lo


def _cum_kernel(ff_ref, b_ref, qa_ref, ka_ref, carry_ref, *, heads):
    @pl.when(pl.program_id(0) == 0)
    def _():
        carry_ref[...] = jnp.zeros_like(carry_ref)

    z = ff_ref[...] + b_ref[...]
    logf = jnp.minimum(z, 0.0) - jnp.log1p(jnp.exp(-jnp.abs(z)))
    tt = z.shape[0]
    row = lax.broadcasted_iota(jnp.int32, (tt, tt), 0)
    col = lax.broadcasted_iota(jnp.int32, (tt, tt), 1)
    tril = jnp.where(col <= row, 1.0, 0.0).astype(BF16)
    local = None
    for piece in _split3(logf):
        part = jnp.dot(tril, piece, preferred_element_type=F32)
        local = part if local is None else local + part
    cum = local + carry_ref[0:1, :]
    carry_ref[...] = jnp.broadcast_to(cum[tt - 1:tt, :], carry_ref.shape)

    c2 = cum * LOG2E
    lane = lax.broadcasted_iota(jnp.int32, (tt, LANES), 1)
    for h in range(heads):
        hi, mid, lo = [p.astype(F32) for p in _split3(c2[:, h:h + 1])]
        qa = jnp.where(lane == 0, hi, jnp.where(lane == 1, mid, jnp.where(lane == 2, lo,
                       jnp.where(lane < 6, 1.0, 0.0))))
        ka = jnp.where(lane < 3, 1.0, jnp.where(lane == 3, -hi, jnp.where(lane == 4, -mid,
                       jnp.where(lane == 5, -lo, 0.0))))
        qa_ref[:, h * LANES:(h + 1) * LANES] = qa.astype(BF16)
        ka_ref[:, h * LANES:(h + 1) * LANES] = ka.astype(BF16)


def _fox_bias_terms(ff, b_pad, heads):
    t, n = ff.shape
    tt = _tile(t, 256)
    out = jax.ShapeDtypeStruct((t, heads * LANES), BF16)
    return pl.pallas_call(
        functools.partial(_cum_kernel, heads=heads),
        out_shape=(out, out),
        grid=(t // tt,),
        in_specs=[pl.BlockSpec((tt, n), lambda i: (i, 0)),
                  pl.BlockSpec((1, n), lambda i: (0, 0))],
        out_specs=(pl.BlockSpec((tt, heads * LANES), lambda i: (i, 0)),
                   pl.BlockSpec((tt, heads * LANES), lambda i: (i, 0))),
        scratch_shapes=[pltpu.VMEM((SUBLANES, n), F32)],
        compiler_params=_params("arbitrary"),
        name="fox_bias_terms",
    )(ff, b_pad)


def _zero_from(x):
    bits = pltpu.bitcast(x, jnp.uint32)
    acc = None
    for r in range(0, bits.shape[0], SUBLANES):
        for c in range(0, bits.shape[1], LANES):
            blk = bits[r:r + SUBLANES, c:c + LANES]
            acc = blk if acc is None else acc | blk
    return (acc >> 16) >> 16


def _tie(x, zero):
    pack = 2 * SUBLANES
    z = jnp.tile(pltpu.bitcast(zero, BF16), (1, x.shape[1] // LANES))
    return jnp.concatenate([x[0:pack, :] + z, x[pack:, :]], axis=0)


V_ROWS = HEAD_DIM + 16


FOX_KEY_GROUPS = 8


def _fox_kernel(q_ref, qa_ref, k_ref, ka_ref, vt_ref, g_ref, o_ref,
                qc_ref, s_ref, mt_ref, m_ref, acc_ref):
    i = pl.program_id(1)
    tq = q_ref.shape[0]
    tk = tq
    hp = vt_ref.shape[0]
    cols = lambda hh: slice(hh * HEAD_DIM, (hh + 1) * HEAD_DIM)
    groups = FOX_KEY_GROUPS if tk % (FOX_KEY_GROUPS * 2 * SUBLANES) == 0 else 1
    gk = tk // groups

    for hh in range(hp):
        qc_ref[hh, :, 0:HEAD_DIM] = q_ref[:, cols(hh)]
        qc_ref[hh, :, HEAD_DIM:2 * HEAD_DIM] = qa_ref[:, cols(hh)]
    m_ref[...] = jnp.full_like(m_ref, -jnp.inf)
    acc_ref[...] = jnp.zeros_like(acc_ref)

    def scores(hh, j, masked, zeros=None):
        k0 = pl.multiple_of(j * tk, tk)
        kc = jnp.concatenate([k_ref[pl.ds(k0, tk), cols(hh)], ka_ref[pl.ds(k0, tk), cols(hh)]], axis=1)
        if zeros is not None:
            kc = jnp.concatenate([_tie(kc[g * gk:(g + 1) * gk], zeros[g]) for g in range(groups)], axis=0)
        s = lax.dot_general(kc, qc_ref[hh], (((1,), (1,)), ((), ())),
                            preferred_element_type=F32)
        if masked:
            key = lax.broadcasted_iota(jnp.int32, (tk, tq), 0)
            qry = lax.broadcasted_iota(jnp.int32, (tk, tq), 1)
            s = jnp.where(key <= qry, s, -jnp.inf)
        s_ref[hh] = s
        mt_ref[hh] = jnp.max(s, axis=0, keepdims=True)

    def step(j, next_masked):
        k0 = pl.multiple_of(j * tk, tk)
        for hh in range(hp):
            m_old = m_ref[hh]
            m_new = jnp.maximum(m_old, mt_ref[hh])
            alpha = jnp.exp2(m_old - m_new)
            ps = [jnp.exp2(s_ref[hh, g * gk:(g + 1) * gk, :] - m_new).astype(BF16) for g in range(groups)]
            if next_masked is not None:
                scores(hh, j + 1, next_masked, [_zero_from(pg) for pg in ps])
            pv = jnp.dot(vt_ref[hh, :, pl.ds(k0, tk)], jnp.concatenate(ps, axis=0),
                         preferred_element_type=F32)
            acc_ref[hh] = alpha * acc_ref[hh] + pv
            m_ref[hh] = m_new

    @pl.when(i == 0)
    def _():
        for hh in range(hp):
            scores(hh, 0, True)

    @pl.when(i > 0)
    def _():
        for hh in range(hp):
            scores(hh, 0, False)

        def body(j, carry):
            step(j, False)
            return carry

        lax.fori_loop(0, i - 1, body, 0)
        step(i - 1, True)

    step(i, None)

    for hh in range(hp):
        acc = acc_ref[hh]
        o = (acc[0:HEAD_DIM, :] / acc[HEAD_DIM:HEAD_DIM + 1, :]).T
        y = o * lax.rsqrt(jnp.mean(o * o, axis=-1, keepdims=True) + EPS) * g_ref[:, cols(hh)]
        o_ref[:, cols(hh)] = y.astype(o_ref.dtype)


def _fox_attention(qkv, qa, ka, vt, g_fox, heads):
    t = qkv.shape[0]
    tq = _tile(t, 512)
    hp = 2 if heads % 2 == 0 else 1
    w = hp * HEAD_DIM
    groups = heads // hp
    per_group = lambda off: pl.BlockSpec((t, w), lambda h, i: (0, off + h))
    q_tile = pl.BlockSpec((tq, w), lambda h, i: (i, h))
    return pl.pallas_call(
        _fox_kernel,
        out_shape=jax.ShapeDtypeStruct((t, heads * HEAD_DIM), BF16),
        grid=(groups, t // tq),
        in_specs=[q_tile, q_tile, per_group(groups), per_group(0),
                  pl.BlockSpec((hp, V_ROWS, t), lambda h, i: (h, 0, 0)),
                  pl.BlockSpec((1, w), lambda h, i: (0, h))],
        out_specs=q_tile,
        scratch_shapes=[pltpu.VMEM((hp, tq, 2 * HEAD_DIM), BF16),
                        pltpu.VMEM((hp, tq, tq), F32), pltpu.VMEM((hp, 1, tq), F32),
                        pltpu.VMEM((hp, 1, tq), F32), pltpu.VMEM((hp, V_ROWS, tq), F32)],
        compiler_params=_params("parallel", "arbitrary"),
        name="fox_attention",
    )(qkv, qa, qkv, ka, vt, g_fox)


def _hgrn_sum_matrix(c):
    levels = int(math.log2(c))
    p = np.arange(c)[:, None]
    j = np.arange(c)[None, :]
    blocks = []
    for lv in range(levels):
        h = c >> (lv + 1)
        second = ((p // h) % 2) == 1
        m_second = (p // h) * h
        m_first = (p // h + 1) * h
        blocks.append(np.where(second, (j >= m_second) & (j <= p), (j > p) & (j < m_first)))
    blocks.append(j <= p)
    blocks.append(j > p)
    return np.concatenate(blocks, axis=0).astype(np.float32)


def _hgrn_kernel(q_ref, f_ref, i_ref, gate_ref, lbl_ref, g_ref, nmat_ref, o_ref, st_ref, *, chunk, layer):
    tt = q_ref.shape[0]
    levels = int(math.log2(chunk))
    pair = 2 * chunk
    nt = lambda a, b: lax.dot_general(a, b, (((1,), (1,)), ((), ())), preferred_element_type=F32)

    @pl.when(pl.program_id(1) == 0)
    def _():
        st_ref[...] = jnp.zeros_like(st_ref)

    lbl = lbl_ref[...]
    e = jnp.exp(lbl - jnp.max(lbl, axis=0, keepdims=True))
    lb = jnp.sum(e[0:layer + 1, :], axis=0, keepdims=True) / jnp.sum(e, axis=0, keepdims=True)

    rowp = lax.broadcasted_iota(jnp.int32, (pair, HEAD_DIM), 0)
    tpos = lax.broadcasted_iota(jnp.int32, (pair, pair), 0)
    spos = lax.broadcasted_iota(jnp.int32, (pair, pair), 1)
    xor = jnp.bitwise_xor(tpos, spos)
    level_of = jnp.where((spos < tpos) & (xor < chunk), 0, -1)
    for lv in range(1, levels):
        level_of = jnp.where((spos < tpos) & (xor < (chunk >> lv)), lv, level_of)
    nmat = nmat_ref[...]

    fg = lb + (1.0 - lb) * jax.nn.sigmoid(f_ref[...])
    logf = jnp.log(fg)
    kk = 1.0 - fg
    qx = q_ref[...]
    qf = qx * jax.nn.sigmoid(qx)
    iv = i_ref[...]
    hi = logf.astype(BF16)
    lo = (logf - hi.astype(F32)).astype(BF16)

    dec = []
    for r0 in range(0, tt, pair):
        r1, r2 = r0 + chunk, r0 + pair
        rhs = jnp.concatenate([jnp.concatenate([hi[r0:r1], lo[r0:r1]], axis=0),
                               jnp.concatenate([hi[r1:r2], lo[r1:r2]], axis=0)], axis=1)
        dec.append(jnp.exp(jnp.dot(nmat, rhs, preferred_element_type=F32)))

    def block(p, blk):
        d = dec[p][blk * chunk:(blk + 1) * chunk]
        return jnp.concatenate([d[:, :HEAD_DIM], d[:, HEAD_DIM:]], axis=0)

    npairs = tt // pair
    rows = [slice(p * pair, (p + 1) * pair) for p in range(npairs)]
    a = [jnp.where(xor == 0, nt(qf[r].astype(BF16), kk[r].astype(BF16)), 0.0) for r in rows]
    for lv in range(levels):
        second = jnp.bitwise_and(rowp, chunk >> (lv + 1)) != 0
        for p, r in enumerate(rows):
            xl = (jnp.where(second, qf[r], kk[r]) * block(p, lv)).astype(BF16)
            a[p] = jnp.where(level_of == lv, nt(xl, xl), a[p])
    o_intra = [jnp.dot(a[p].astype(BF16), iv[r], preferred_element_type=F32) for p, r in enumerate(rows)]
    d_inc = jnp.concatenate([block(p, levels) for p in range(npairs)], axis=0)
    d_suf = jnp.concatenate([block(p, levels + 1) for p in range(npairs)], axis=0)
    qd = (qf * d_inc).astype(BF16)
    kd = (kk * d_suf).astype(BF16)
    upd = [lax.dot_general(iv[c:c + chunk], kd[c:c + chunk], (((0,), (0,)), ((), ())),
                           preferred_element_type=F32) for c in range(0, tt, chunk)]

    st = st_ref[...]
    o_inter = []
    for n, c in enumerate(range(0, tt, chunk)):
        o_inter.append(nt(qd[c:c + chunk], st.astype(BF16)))
        st = d_inc[c + chunk - 1:c + chunk, :] * st + upd[n]
    st_ref[...] = st

    o = jnp.concatenate(o_intra, axis=0) + jnp.concatenate(o_inter, axis=0)
    y = o * lax.rsqrt(jnp.mean(o * o, axis=-1, keepdims=True) + EPS) * g_ref[...]
    gx = gate_ref[...]
    o_ref[...] = (y * (gx * jax.nn.sigmoid(gx))).astype(o_ref.dtype)


def _hgrn2(qf_proj, i_proj, g_proj, lb_logits, g_out, heads, layer):
    t = i_proj.shape[0]
    chunk = min(HGRN_CHUNK, t)
    tt = _tile(t, 16 * chunk)
    sums = _hgrn_sum_matrix(chunk)
    nmat = jnp.asarray(np.concatenate([sums, sums], axis=1), dtype=BF16)
    nl = lb_logits.shape[0]
    blk = lambda off: pl.BlockSpec((tt, HEAD_DIM), lambda h, i: (i, off + h))
    return pl.pallas_call(
        functools.partial(_hgrn_kernel, chunk=chunk, layer=layer),
        out_shape=jax.ShapeDtypeStruct((t, heads * HEAD_DIM), BF16),
        grid=(heads, t // tt),
        in_specs=[blk(0), blk(heads), blk(0), blk(0),
                  pl.BlockSpec((nl, HEAD_DIM), lambda h, i: (0, h)),
                  pl.BlockSpec((1, HEAD_DIM), lambda h, i: (0, h)),
                  pl.BlockSpec(nmat.shape, lambda h, i: (0, 0))],
        out_specs=blk(0),
        scratch_shapes=[pltpu.VMEM((HEAD_DIM, HEAD_DIM), F32)],
        compiler_params=_params("parallel", "arbitrary"),
        name="hgrn2",
    )(qf_proj, qf_proj, i_proj, g_proj, lb_logits, g_out, nmat)


MXU_DEPTH = 256


def _ffn_up_kernel(h_ref, wa_ref, wv_ref, cwa_ref, cwv_ref, cba_ref, cbv_ref, o_ref,
                   wab_ref, wvb_ref, ua_ref, uv_ref, *, n_row_tiles, n_steps):
    tm, d = h_ref.shape
    halo = SUBLANES
    slabs = d // MXU_DEPTH if d % MXU_DEPTH == 0 and tm % (d // MXU_DEPTH * SUBLANES) == 0 else 1
    sub = tm // slabs
    s = pl.program_id(0)
    row_tile = jnp.minimum(s, n_steps - 1) % n_row_tiles
    cur = s % 2
    prev = 1 - cur

    @pl.when(s == 0)
    def _():
        ua_ref[1] = jnp.zeros(ua_ref.shape[1:], F32)
        uv_ref[1] = jnp.zeros(uv_ref.shape[1:], F32)

    @pl.when(jnp.logical_and(row_tile == 0, s < n_steps))
    def _():
        wab_ref[...] = wa_ref[...].astype(BF16)
        wvb_ref[...] = wv_ref[...].astype(BF16)

    zeros = []
    for r in range(slabs):
        base = halo + r * sub

        def conv(cw_ref, cb_ref, u_ref):
            y = cb_ref[...] + cw_ref[0:1, :] * u_ref[prev, base - 2:base - 2 + sub, :]
            y = y + cw_ref[1:2, :] * u_ref[prev, base - 1:base - 1 + sub, :]
            return y + cw_ref[2:3, :] * u_ref[prev, base:base + sub, :]

        ya = conv(cwa_ref, cba_ref, ua_ref)
        yv = conv(cwv_ref, cbv_ref, uv_ref)
        act = ya * jax.nn.sigmoid(ya) * yv
        o_ref[r * sub:(r + 1) * sub, :] = act.astype(o_ref.dtype)
        zeros.append(_zero_from(act))

    first = row_tile == 0
    kw = d // slabs
    lhs = jnp.concatenate([_tie(h_ref[:, k * kw:(k + 1) * kw], zeros[k]) for k in range(slabs)], axis=1)
    for w_ref, u_ref in ((wab_ref, ua_ref), (wvb_ref, uv_ref)):
        u_ref[cur, 0:halo, :] = jnp.where(first, 0.0, u_ref[prev, tm:tm + halo, :])
        u_ref[cur, halo:halo + tm, :] = jnp.dot(lhs, w_ref[...], preferred_element_type=F32)


def _ffn_up(h, w_up, conv_w, conv_b):
    t, d = h.shape
    dff = w_up.shape[1] // 2
    tm = _tile(t, 1024)
    tn = _tile(dff, 256)
    nj = dff // tn
    ni = t // tm
    n_steps = nj * ni
    cur = lambda s: jnp.minimum(s, n_steps - 1)
    lag = lambda s: jnp.maximum(s - 1, 0)
    return pl.pallas_call(
        functools.partial(_ffn_up_kernel, n_row_tiles=ni, n_steps=n_steps),
        out_shape=jax.ShapeDtypeStruct((t, dff), BF16),
        grid=(n_steps + 1,),
        in_specs=[pl.BlockSpec((tm, d), lambda s: (cur(s) % ni, 0)),
                  pl.BlockSpec((d, tn), lambda s: (0, cur(s) // ni)),
                  pl.BlockSpec((d, tn), lambda s: (0, nj + cur(s) // ni)),
                  pl.BlockSpec((CONV_WIDTH, tn), lambda s: (0, lag(s) // ni)),
                  pl.BlockSpec((CONV_WIDTH, tn), lambda s: (0, nj + lag(s) // ni)),
                  pl.BlockSpec((1, tn), lambda s: (0, lag(s) // ni)),
                  pl.BlockSpec((1, tn), lambda s: (0, nj + lag(s) // ni))],
        out_specs=pl.BlockSpec((tm, tn), lambda s: (lag(s) % ni, lag(s) // ni)),
        scratch_shapes=[pltpu.VMEM((d, tn), BF16), pltpu.VMEM((d, tn), BF16),
                        pltpu.VMEM((2, tm + SUBLANES, tn), F32), pltpu.VMEM((2, tm + SUBLANES, tn), F32)],
        compiler_params=_params("arbitrary"),
        name="ffn_up_conv_glu",
    )(h, w_up, w_up, conv_w, conv_w, conv_b, conv_b)


def _ffn_down_kernel(a_ref, w_ref, x_ref, g_ref, o_ref):
    acc = jnp.dot(a_ref[...], w_ref[...], preferred_element_type=F32)
    o_ref[...] = x_ref[...] + g_ref[...] * acc


def _ffn_down(act, w_down_bf16, x, gate):
    t, k = act.shape
    n = w_down_bf16.shape[1]
    tm = _tile(t, 512)
    tn = _tile(n, 256)
    return pl.pallas_call(
        _ffn_down_kernel,
        out_shape=jax.ShapeDtypeStruct((t, n), F32),
        grid=(t // tm, n // tn),
        in_specs=[pl.BlockSpec((tm, k), lambda i, j: (i, 0)),
                  pl.BlockSpec((k, tn), lambda i, j: (0, j)),
                  pl.BlockSpec((tm, tn), lambda i, j: (i, j)),
                  pl.BlockSpec((1, tn), lambda i, j: (0, j))],
        out_specs=pl.BlockSpec((tm, tn), lambda i, j: (i, j)),
        compiler_params=_params("parallel", "parallel"),
        name="ffn_down",
    )(act, w_down_bf16, x, gate)


def kernel(x, c, w_ada, b_ada, g_mix_norm, w_in, b_fox_f, hgrn_lb_logits, g_fox_out, g_hgrn_out,
           w_out, g_ffn_norm, w_up, conv_w, conv_b, w_down, g_final):
    b, t, d = x.shape
    assert b == 1, "single-sequence layer"
    depth = w_ada.shape[0]
    fox_heads = b_fox_f.shape[1]
    fox_width = g_fox_out.shape[1]
    hg_kwidth = hgrn_lb_logits.shape[1]
    hg_vwidth = g_hgrn_out.shape[1]
    hg_heads = hg_vwidth // HEAD_DIM
    assert fox_width == fox_heads * HEAD_DIM and hg_kwidth == hg_heads * HEAD_DIM
    assert fox_heads <= LANES
    fox_f0 = 3 * fox_width
    hg_q0 = fox_f0 + fox_heads

    xs = x.reshape(t, d)
    row = lambda v: v.reshape(1, -1)
    for l in range(depth):
        mod = _ada_mod(c, w_ada[l], b_ada[l])
        sh1, sc1, gt1, sh2, sc2, gt2 = [mod[:, k * d:(k + 1) * d] for k in range(N_MOD)]

        h1 = _norm_mod(xs, row(g_mix_norm[l]), sc1, sh1, BF16)
        w_t = jnp.swapaxes(w_in[l], 0, 1)
        b_ff = jnp.pad(b_fox_f[l], (0, LANES - fox_heads)).reshape(1, LANES)
        in_proj = functools.partial(_proj, [h1], w_t, transposed=True)

        qkv = in_proj(0, 3 * fox_width, BF16, lead_cols=fox_width,
                      lead_scale=LOG2E * HEAD_DIM ** -0.5, name="proj_fox_qkv")
        ff = in_proj(fox_f0, LANES, F32, name="proj_fox_f")
        hqf = in_proj(hg_q0, 2 * hg_kwidth, F32, name="proj_hgrn_qf")
        hi = in_proj(hg_q0 + 2 * hg_kwidth, hg_vwidth, BF16, name="proj_hgrn_i")
        hg = in_proj(hg_q0 + 2 * hg_kwidth + hg_vwidth, hg_vwidth, F32, name="proj_hgrn_g")

        qa, ka = _fox_bias_terms(ff, b_ff, fox_heads)
        vt = qkv[:, 2 * fox_width:].reshape(t, fox_heads, HEAD_DIM).transpose(1, 2, 0)
        vt = jnp.concatenate([vt, jnp.ones((fox_heads, V_ROWS - HEAD_DIM, t), BF16)], axis=1)
        o_fox = _fox_attention(qkv, qa, ka, vt, row(g_fox_out[l]), fox_heads)
        o_hg = _hgrn2(hqf, hi, hg, hgrn_lb_logits, row(g_hgrn_out[l]), hg_heads, l)

        xs = _proj([o_fox, o_hg], w_out[l], 0, d, F32, res=xs, gate=gt1, name="proj_out")

        h2 = _norm_mod(xs, row(g_ffn_norm[l]), sc2, sh2, BF16)
        act = _ffn_up(h2, w_up[l], conv_w[l], row(conv_b[l]))
        xs = _ffn_down(act, w_down[l].astype(BF16), xs, gt2)

    out = _norm_mod(xs, row(g_final), None, None, x.dtype)
    return out.reshape(b, t, d)
```

```python
import functools
import math

import numpy as np
import jax
import jax.numpy as jnp
from jax import lax
from jax.experimental import pallas as pl
from jax.experimental.pallas import tpu as pltpu

F32 = jnp.float32
BF16 = jnp.bfloat16

EPS = 1e-6
CONV_WIDTH = 3
N_MOD = 6
HEAD_DIM = 128
HGRN_CHUNK = 64
LANES = 128
SUBLANES = 8
VMEM_LIMIT = 56 * 1024 * 1024
LOG2E = 1.4426950408889634


def _params(*sem):
    return pltpu.CompilerParams(dimension_semantics=sem, vmem_limit_bytes=VMEM_LIMIT)


def _tile(n, pref):
    if n <= pref:
        return n
    t = pref
    while n % t:
        t //= 2
    return t


def _ada_kernel(cb_ref, w_ref, b_ref, o_ref, cond_ref):
    @pl.when(pl.program_id(0) == 0)
    def _():
        cb = cb_ref[...]
        cond_ref[...] = cb * jax.nn.sigmoid(cb)

    tn = o_ref.shape[1]
    for j in range(tn // LANES):
        sl = slice(j * LANES, (j + 1) * LANES)
        col = jnp.sum(w_ref[:, sl] * cond_ref[...], axis=0, keepdims=True)
        o_ref[:, sl] = col + b_ref[:, sl]


def _ada_mod(c, w_ada, b_ada):
    d, n = w_ada.shape
    cb = jnp.broadcast_to(c.reshape(d, 1), (d, LANES))
    tn = _tile(n, 512)
    return pl.pallas_call(
        _ada_kernel,
        out_shape=jax.ShapeDtypeStruct((1, n), F32),
        grid=(n // tn,),
        in_specs=[pl.BlockSpec((d, LANES), lambda j: (0, 0)),
                  pl.BlockSpec((d, tn), lambda j: (0, j)),
                  pl.BlockSpec((1, tn), lambda j: (0, j))],
        out_specs=pl.BlockSpec((1, tn), lambda j: (0, j)),
        scratch_shapes=[pltpu.VMEM((d, LANES), F32)],
        compiler_params=_params("arbitrary"),
        name="ada_mod",
    )(cb, w_ada, b_ada.reshape(1, n))


def _norm_mod_kernel(x_ref, g_ref, sc_ref, sh_ref, o_ref):
    x = x_ref[...]
    y = x * lax.rsqrt(jnp.mean(x * x, axis=-1, keepdims=True) + EPS) * g_ref[...]
    o_ref[...] = (y * (1.0 + sc_ref[...]) + sh_ref[...]).astype(o_ref.dtype)


def _norm_kernel(x_ref, g_ref, o_ref):
    x = x_ref[...]
    y = x * lax.rsqrt(jnp.mean(x * x, axis=-1, keepdims=True) + EPS) * g_ref[...]
    o_ref[...] = y.astype(o_ref.dtype)


def _norm_mod(x, g, sc, sh, out_dtype):
    t, d = x.shape
    tm = _tile(t, 256)
    row = pl.BlockSpec((tm, d), lambda i: (i, 0))
    vec = pl.BlockSpec((1, d), lambda i: (0, 0))
    if sc is None:
        kern, args, specs = _norm_kernel, (x, g), [row, vec]
    else:
        kern, args, specs = _norm_mod_kernel, (x, g, sc, sh), [row, vec, vec, vec]
    return pl.pallas_call(
        kern,
        out_shape=jax.ShapeDtypeStruct((t, d), out_dtype),
        grid=(t // tm,),
        in_specs=specs,
        out_specs=row,
        compiler_params=_params("parallel"),
        name="rms_norm",
    )(*args)


def _proj_kernel(*refs, n_a, has_res, lead_tiles, lead_scale, transposed, out_transposed):
    a_refs = refs[:n_a]
    w_ref = refs[n_a]
    pos = n_a + 1
    if has_res:
        x_ref, g_ref = refs[pos], refs[pos + 1]
        pos += 2
    o_ref, wb_ref = refs[pos], refs[pos + 1]

    @pl.when(pl.program_id(1) == 0)
    def _():
        wb_ref[...] = w_ref[...].astype(BF16)

    if out_transposed:
        o_ref[...] = lax.dot_general(wb_ref[...], a_refs[0][...], (((1,), (1,)), ((), ())),
                                     preferred_element_type=F32).astype(o_ref.dtype)
        return

    acc = None
    k0 = 0
    for a_ref in a_refs:
        ks = a_ref.shape[1]
        if transposed:
            part = lax.dot_general(a_ref[...], wb_ref[:, k0:k0 + ks], (((1,), (1,)), ((), ())),
                                   preferred_element_type=F32)
        else:
            part = jnp.dot(a_ref[...], wb_ref[k0:k0 + ks, :], preferred_element_type=F32)
        acc = part if acc is None else acc + part
        k0 += ks
    if lead_tiles:
        acc = acc * jnp.where(pl.program_id(0) < lead_tiles, lead_scale, 1.0).astype(F32)
    if has_res:
        acc = x_ref[...] + g_ref[...] * acc
    o_ref[...] = acc.astype(o_ref.dtype)


def _proj(a_list, w, col0, n, out_dtype, res=None, gate=None, lead_cols=0, lead_scale=1.0,
          transposed=False, out_transposed=False, name="proj"):
    m = a_list[0].shape[0]
    k = w.shape[1] if transposed else w.shape[0]
    assert sum(a.shape[1] for a in a_list) == k
    tm = _tile(m, 1024)
    tn = _tile(n, 512)
    assert lead_cols % tn == 0
    assert not out_transposed or (transposed and len(a_list) == 1 and res is None and not lead_cols)
    in_specs = [pl.BlockSpec((tm, a.shape[1]), lambda j, i: (i, 0)) for a in a_list]
    if transposed:
        assert col0 % SUBLANES == 0
        in_specs.append(pl.BlockSpec((pl.Element(tn), pl.Element(k)),
                                     lambda j, i: (pl.multiple_of(col0 + j * tn, SUBLANES), 0)))
        wb_shape = (tn, k)
    else:
        assert col0 % tn == 0
        in_specs.append(pl.BlockSpec((k, tn), lambda j, i: (0, col0 // tn + j)))
        wb_shape = (k, tn)
    args = list(a_list) + [w]
    if res is not None:
        in_specs += [pl.BlockSpec((tm, tn), lambda j, i: (i, j)),
                     pl.BlockSpec((1, tn), lambda j, i: (0, j))]
        args += [res, gate]
    return pl.pallas_call(
        functools.partial(_proj_kernel, n_a=len(a_list), has_res=res is not None,
                          lead_tiles=lead_cols // tn, lead_scale=lead_scale, transposed=transposed,
                          out_transposed=out_transposed),
        out_shape=jax.ShapeDtypeStruct((n, m) if out_transposed else (m, n), out_dtype),
        grid=(n // tn, m // tm),
        in_specs=in_specs,
        out_specs=(pl.BlockSpec((tn, tm), lambda j, i: (j, i)) if out_transposed
                   else pl.BlockSpec((tm, tn), lambda j, i: (i, j))),
        scratch_shapes=[pltpu.VMEM(wb_shape, BF16)],
        compiler_params=_params("parallel", "arbitrary"),
        name=name,
    )(*args)


def _split3(x):
    hi = x.astype(BF16)
    r1 = x - hi.astype(F32)
    mid = r1.astype(BF16)
    lo = (r1 - mid.astype(F32)).astype(BF16)
    return hi, mid, lo


def _bias_scatter(heads):
    w = heads * LANES
    scat = np.zeros((3 * LANES, 2 * w), np.float32)
    const = np.zeros((1, 2 * w), np.float32)
    for h in range(heads):
        for piece in range(3):
            scat[piece * LANES + h, h * LANES + piece] = 1.0
            scat[piece * LANES + h, w + h * LANES + 3 + piece] = -1.0
            const[0, h * LANES + 3 + piece] = 1.0
            const[0, w + h * LANES + piece] = 1.0
    return scat, const


def _emit_bias_terms(ff, b_ref, scat_ref, const_ref, qa_ref, ka_ref, carry_ref):
    z = ff + b_ref[...]
    logf = jnp.minimum(z, 0.0) - jnp.log1p(jnp.exp(-jnp.abs(z)))
    tt = z.shape[0]
    row = lax.broadcasted_iota(jnp.int32, (tt, tt), 0)
    col = lax.broadcasted_iota(jnp.int32, (tt, tt), 1)
    tril = jnp.where(col <= row, 1.0, 0.0).astype(BF16)
    local = None
    for piece in _split3(logf):
        part = jnp.dot(tril, piece, preferred_element_type=F32)
        local = part if local is None else local + part
    cum = local + carry_ref[0:1, :]
    carry_ref[...] = jnp.broadcast_to(cum[tt - 1:tt, :], carry_ref.shape)

    pieces = jnp.concatenate(_split3(cum * LOG2E), axis=1)
    terms = jnp.dot(pieces, scat_ref[...], preferred_element_type=F32) + const_ref[...]
    w = qa_ref.shape[1]
    qa_ref[...] = terms[:, :w].astype(BF16)
    ka_ref[...] = terms[:, w:].astype(BF16)


def _norm_fox_kernel(x_ref, g_ref, sc_ref, sh_ref, wf_ref, b_ref, scat_ref, const_ref,
                     o_ref, qa_ref, ka_ref, wfb_ref, carry_ref):
    @pl.when(pl.program_id(0) == 0)
    def _():
        wfb_ref[...] = wf_ref[...].astype(BF16)
        carry_ref[...] = jnp.zeros_like(carry_ref)

    x = x_ref[...]
    y = x * lax.rsqrt(jnp.mean(x * x, axis=-1, keepdims=True) + EPS) * g_ref[...]
    h = (y * (1.0 + sc_ref[...]) + sh_ref[...]).astype(o_ref.dtype)
    o_ref[...] = h
    ff = lax.dot_general(h, wfb_ref[...], (((1,), (1,)), ((), ())), preferred_element_type=F32)
    _emit_bias_terms(ff, b_ref, scat_ref, const_ref, qa_ref, ka_ref, carry_ref)


def _norm_mod_fox(x, g, sc, sh, w_t, f_row0, b_pad, heads):
    t, d = x.shape
    tm = _tile(t, 512)
    row = pl.BlockSpec((tm, d), lambda i: (i, 0))
    vec = pl.BlockSpec((1, d), lambda i: (0, 0))
    bias = jax.ShapeDtypeStruct((t, heads * LANES), BF16)
    bias_spec = pl.BlockSpec((tm, heads * LANES), lambda i: (i, 0))
    scat, const = _bias_scatter(heads)
    return pl.pallas_call(
        _norm_fox_kernel,
        out_shape=(jax.ShapeDtypeStruct((t, d), BF16), bias, bias),
        grid=(t // tm,),
        in_specs=[row, vec, vec, vec,
                  pl.BlockSpec((pl.Element(LANES), pl.Element(d)), lambda i: (f_row0, 0)),
                  pl.BlockSpec((1, LANES), lambda i: (0, 0)),
                  pl.BlockSpec(scat.shape, lambda i: (0, 0)),
                  pl.BlockSpec(const.shape, lambda i: (0, 0))],
        out_specs=(row, bias_spec, bias_spec),
        scratch_shapes=[pltpu.VMEM((LANES, d), BF16), pltpu.VMEM((SUBLANES, LANES), F32)],
        compiler_params=_params("arbitrary"),
        name="rms_norm_fox_bias",
    )(x, g, sc, sh, w_t, b_pad, jnp.asarray(scat, BF16), jnp.asarray(const, F32))


def _zero_from(x):
    bits = pltpu.bitcast(x, jnp.uint32)
    acc = None
    for r in range(0, bits.shape[0], SUBLANES):
        for c in range(0, bits.shape[1], LANES):
            blk = bits[r:r + SUBLANES, c:c + LANES]
            acc = blk if acc is None else acc | blk
    return (acc >> 16) >> 16


def _tie(x, zero):
    pack = 2 * SUBLANES
    z = jnp.tile(pltpu.bitcast(zero, BF16), (1, x.shape[1] // LANES))
    return jnp.concatenate([x[0:pack, :] + z, x[pack:, :]], axis=0)


V_ROWS = HEAD_DIM + 16


FOX_KEY_GROUPS = 8


def _fox_kernel(q_ref, qa_ref, k_ref, ka_ref, vt_ref, g_ref, o_ref,
                qc_ref, s_ref, mt_ref, m_ref, acc_ref):
    i = pl.program_id(1)
    tq = q_ref.shape[0]
    tk = tq
    hp = vt_ref.shape[0] // HEAD_DIM
    cols = lambda hh: slice(hh * HEAD_DIM, (hh + 1) * HEAD_DIM)
    ones = jnp.ones((V_ROWS - HEAD_DIM, tk), BF16)
    groups = FOX_KEY_GROUPS if tk % (FOX_KEY_GROUPS * 2 * SUBLANES) == 0 else 1
    gk = tk // groups

    for hh in range(hp):
        qc_ref[hh, :, 0:HEAD_DIM] = q_ref[:, cols(hh)]
        qc_ref[hh, :, HEAD_DIM:2 * HEAD_DIM] = qa_ref[:, cols(hh)]
    m_ref[...] = jnp.full_like(m_ref, -jnp.inf)
    acc_ref[...] = jnp.zeros_like(acc_ref)

    def scores(hh, j, masked, zeros=None):
        k0 = pl.multiple_of(j * tk, tk)
        kc = jnp.concatenate([k_ref[pl.ds(k0, tk), cols(hh)], ka_ref[pl.ds(k0, tk), cols(hh)]], axis=1)
        if zeros is not None:
            kc = jnp.concatenate([_tie(kc[g * gk:(g + 1) * gk], zeros[g]) for g in range(groups)], axis=0)
        s = lax.dot_general(kc, qc_ref[hh], (((1,), (1,)), ((), ())),
                            preferred_element_type=F32)
        if masked:
            key = lax.broadcasted_iota(jnp.int32, (tk, tq), 0)
            qry = lax.broadcasted_iota(jnp.int32, (tk, tq), 1)
            s = jnp.where(key <= qry, s, -jnp.inf)
        s_ref[hh] = s
        mt_ref[hh] = jnp.max(s, axis=0, keepdims=True)

    def step(j, next_masked):
        k0 = pl.multiple_of(j * tk, tk)
        for hh in range(hp):
            m_old = m_ref[hh]
            m_new = jnp.maximum(m_old, mt_ref[hh])
            alpha = jnp.exp2(m_old - m_new)
            ps = [jnp.exp2(s_ref[hh, g * gk:(g + 1) * gk, :] - m_new).astype(BF16) for g in range(groups)]
            if next_masked is not None:
                scores(hh, j + 1, next_masked, [_zero_from(pg) for pg in ps])
            vt = jnp.concatenate([vt_ref[cols(hh), pl.ds(k0, tk)], ones], axis=0)
            pv = jnp.dot(vt, jnp.concatenate(ps, axis=0), preferred_element_type=F32)
            acc_ref[hh] = alpha * acc_ref[hh] + pv
            m_ref[hh] = m_new

    @pl.when(i == 0)
    def _():
        for hh in range(hp):
            scores(hh, 0, True)

    @pl.when(i > 0)
    def _():
        for hh in range(hp):
            scores(hh, 0, False)

        def body(j, carry):
            step(j, False)
            return carry

        lax.fori_loop(0, i - 1, body, 0)
        step(i - 1, True)

    step(i, None)

    for hh in range(hp):
        acc = acc_ref[hh]
        o = (acc[0:HEAD_DIM, :] / acc[HEAD_DIM:HEAD_DIM + 1, :]).T
        y = o * lax.rsqrt(jnp.mean(o * o, axis=-1, keepdims=True) + EPS) * g_ref[:, cols(hh)]
        o_ref[:, cols(hh)] = y.astype(o_ref.dtype)


def _fox_attention(qk, qa, ka, vt, g_fox, heads):
    t = qk.shape[0]
    tq = _tile(t, 512)
    hp = 2 if heads % 2 == 0 else 1
    w = hp * HEAD_DIM
    groups = heads // hp
    per_group = lambda off: pl.BlockSpec((t, w), lambda h, i: (0, off + h))
    q_tile = pl.BlockSpec((tq, w), lambda h, i: (i, h))
    return pl.pallas_call(
        _fox_kernel,
        out_shape=jax.ShapeDtypeStruct((t, heads * HEAD_DIM), BF16),
        grid=(groups, t // tq),
        in_specs=[q_tile, q_tile, per_group(groups), per_group(0),
                  pl.BlockSpec((w, t), lambda h, i: (h, 0)),
                  pl.BlockSpec((1, w), lambda h, i: (0, h))],
        out_specs=q_tile,
        scratch_shapes=[pltpu.VMEM((hp, tq, 2 * HEAD_DIM), BF16),
                        pltpu.VMEM((hp, tq, tq), F32), pltpu.VMEM((hp, 1, tq), F32),
                        pltpu.VMEM((hp, 1, tq), F32), pltpu.VMEM((hp, V_ROWS, tq), F32)],
        compiler_params=_params("parallel", "arbitrary"),
        name="fox_attention",
    )(qk, qa, qk, ka, vt, g_fox)


def _hgrn_sum_matrix(c):
    levels = int(math.log2(c))
    p = np.arange(c)[:, None]
    j = np.arange(c)[None, :]
    blocks = []
    for lv in range(levels):
        h = c >> (lv + 1)
        second = ((p // h) % 2) == 1
        m_second = (p // h) * h
        m_first = (p // h + 1) * h
        blocks.append(np.where(second, (j >= m_second) & (j <= p), (j > p) & (j < m_first)))
    blocks.append(j <= p)
    blocks.append(j > p)
    return np.concatenate(blocks, axis=0).astype(np.float32)


def _hgrn_kernel(q_ref, f_ref, i_ref, gate_ref, lbl_ref, g_ref, nmat_ref, o_ref, st_ref, *, chunk, layer):
    tt = q_ref.shape[0]
    levels = int(math.log2(chunk))
    pair = 2 * chunk
    nt = lambda a, b: lax.dot_general(a, b, (((1,), (1,)), ((), ())), preferred_element_type=F32)

    @pl.when(pl.program_id(1) == 0)
    def _():
        st_ref[...] = jnp.zeros_like(st_ref)

    lbl = lbl_ref[...]
    e = jnp.exp(lbl - jnp.max(lbl, axis=0, keepdims=True))
    lb = jnp.sum(e[0:layer + 1, :], axis=0, keepdims=True) / jnp.sum(e, axis=0, keepdims=True)

    rowp = lax.broadcasted_iota(jnp.int32, (pair, HEAD_DIM), 0)
    tpos = lax.broadcasted_iota(jnp.int32, (pair, pair), 0)
    spos = lax.broadcasted_iota(jnp.int32, (pair, pair), 1)
    xor = jnp.bitwise_xor(tpos, spos)
    level_of = jnp.where((spos < tpos) & (xor < chunk), 0, -1)
    for lv in range(1, levels):
        level_of = jnp.where((spos < tpos) & (xor < (chunk >> lv)), lv, level_of)
    nmat = nmat_ref[...]

    npairs = tt // pair
    qf, kk, dec, a = {}, {}, {}, {}
    st = st_ref[...]

    def block(p, blk):
        d = dec[p][blk * chunk:(blk + 1) * chunk]
        return jnp.concatenate([d[:, :HEAD_DIM], d[:, HEAD_DIM:]], axis=0)

    def decay_stage(p):
        r = slice(p * pair, (p + 1) * pair)
        fg = lb + (1.0 - lb) * jax.nn.sigmoid(f_ref[r, :])
        logf = jnp.log(fg)
        kk[p] = 1.0 - fg
        qx = q_ref[r, :]
        qf[p] = qx * jax.nn.sigmoid(qx)
        hi = logf.astype(BF16)
        lo = (logf - hi.astype(F32)).astype(BF16)
        rhs = jnp.concatenate([jnp.concatenate([hi[:chunk], lo[:chunk]], axis=0),
                               jnp.concatenate([hi[chunk:], lo[chunk:]], axis=0)], axis=1)
        dec[p] = jnp.exp(jnp.dot(nmat, rhs, preferred_element_type=F32))

    def intra_stage(p):
        acc = jnp.where(xor == 0, nt(qf[p].astype(BF16), kk[p].astype(BF16)), 0.0)
        for lv in range(levels):
            second = jnp.bitwise_and(rowp, chunk >> (lv + 1)) != 0
            xl = (jnp.where(second, qf[p], kk[p]) * block(p, lv)).astype(BF16)
            acc = jnp.where(level_of == lv, nt(xl, xl), acc)
        a[p] = acc.astype(BF16)

    def state_stage(p, st):
        r = slice(p * pair, (p + 1) * pair)
        d_inc, d_suf = block(p, levels), block(p, levels + 1)
        qd = (qf[p] * d_inc).astype(BF16)
        kd = (kk[p] * d_suf).astype(BF16)
        ivp = i_ref[r, :]
        o_intra = jnp.dot(a[p], ivp, preferred_element_type=F32)
        o_inter = []
        for c in range(0, pair, chunk):
            upd = lax.dot_general(ivp[c:c + chunk], kd[c:c + chunk], (((0,), (0,)), ((), ())),
                                  preferred_element_type=F32)
            o_inter.append(nt(qd[c:c + chunk], st.astype(BF16)))
            st = d_inc[c + chunk - 1:c + chunk, :] * st + upd
        o = o_intra + jnp.concatenate(o_inter, axis=0)
        y = o * lax.rsqrt(jnp.mean(o * o, axis=-1, keepdims=True) + EPS) * g_ref[...]
        gx = gate_ref[r, :]
        o_ref[r, :] = (y * (gx * jax.nn.sigmoid(gx))).astype(o_ref.dtype)
        return st

    for t in range(npairs + 2):
        if t < npairs:
            decay_stage(t)
        if 1 <= t <= npairs:
            intra_stage(t - 1)
        if t >= 2:
            st = state_stage(t - 2, st)
    st_ref[...] = st


def _hgrn2(qf_proj, i_proj, g_proj, lb_logits, g_out, heads, layer):
    t = i_proj.shape[0]
    chunk = min(HGRN_CHUNK, t)
    tt = _tile(t, 16 * chunk)
    sums = _hgrn_sum_matrix(chunk)
    nmat = jnp.asarray(np.concatenate([sums, sums], axis=1), dtype=BF16)
    nl = lb_logits.shape[0]
    blk = lambda off: pl.BlockSpec((tt, HEAD_DIM), lambda h, i: (i, off + h))
    return pl.pallas_call(
        functools.partial(_hgrn_kernel, chunk=chunk, layer=layer),
        out_shape=jax.ShapeDtypeStruct((t, heads * HEAD_DIM), BF16),
        grid=(heads, t // tt),
        in_specs=[blk(0), blk(heads), blk(0), blk(0),
                  pl.BlockSpec((nl, HEAD_DIM), lambda h, i: (0, h)),
                  pl.BlockSpec((1, HEAD_DIM), lambda h, i: (0, h)),
                  pl.BlockSpec(nmat.shape, lambda h, i: (0, 0))],
        out_specs=blk(0),
        scratch_shapes=[pltpu.VMEM((HEAD_DIM, HEAD_DIM), F32)],
        compiler_params=_params("parallel", "arbitrary"),
        name="hgrn2",
    )(qf_proj, qf_proj, i_proj, g_proj, lb_logits, g_out, nmat)


MXU_DEPTH = 256


def _ffn_up_kernel(h_ref, wa_ref, wv_ref, cwa_ref, cwv_ref, cba_ref, cbv_ref, o_ref,
                   wab_ref, wvb_ref, ua_ref, uv_ref, *, n_row_tiles, n_steps):
    tm, d = h_ref.shape
    halo = SUBLANES
    slabs = d // MXU_DEPTH if d % MXU_DEPTH == 0 and tm % (d // MXU_DEPTH * SUBLANES) == 0 else 1
    sub = tm // slabs
    s = pl.program_id(0)
    row_tile = jnp.minimum(s, n_steps - 1) % n_row_tiles
    cur = s % 2
    prev = 1 - cur

    @pl.when(s == 0)
    def _():
        ua_ref[1] = jnp.zeros(ua_ref.shape[1:], F32)
        uv_ref[1] = jnp.zeros(uv_ref.shape[1:], F32)

    @pl.when(jnp.logical_and(row_tile == 0, s < n_steps))
    def _():
        wab_ref[...] = wa_ref[...].astype(BF16)
        wvb_ref[...] = wv_ref[...].astype(BF16)

    zeros = []
    for r in range(slabs):
        base = halo + r * sub

        def conv(cw_ref, cb_ref, u_ref):
            y = cb_ref[...] + cw_ref[0:1, :] * u_ref[prev, base - 2:base - 2 + sub, :]
            y = y + cw_ref[1:2, :] * u_ref[prev, base - 1:base - 1 + sub, :]
            return y + cw_ref[2:3, :] * u_ref[prev, base:base + sub, :]

        ya = conv(cwa_ref, cba_ref, ua_ref)
        yv = conv(cwv_ref, cbv_ref, uv_ref)
        act = ya * jax.nn.sigmoid(ya) * yv
        o_ref[r * sub:(r + 1) * sub, :] = act.astype(o_ref.dtype)
        zeros.append(_zero_from(act))

    first = row_tile == 0
    kw = d // slabs
    lhs = jnp.concatenate([_tie(h_ref[:, k * kw:(k + 1) * kw], zeros[k]) for k in range(slabs)], axis=1)
    for w_ref, u_ref in ((wab_ref, ua_ref), (wvb_ref, uv_ref)):
        u_ref[cur, 0:halo, :] = jnp.where(first, 0.0, u_ref[prev, tm:tm + halo, :])
        u_ref[cur, halo:halo + tm, :] = jnp.dot(lhs, w_ref[...], preferred_element_type=F32)


def _ffn_up(h, w_up, conv_w, conv_b):
    t, d = h.shape
    dff = w_up.shape[1] // 2
    tm = _tile(t, 1024)
    tn = _tile(dff, 256)
    nj = dff // tn
    ni = t // tm
    n_steps = nj * ni
    cur = lambda s: jnp.minimum(s, n_steps - 1)
    lag = lambda s: jnp.maximum(s - 1, 0)
    return pl.pallas_call(
        functools.partial(_ffn_up_kernel, n_row_tiles=ni, n_steps=n_steps),
        out_shape=jax.ShapeDtypeStruct((t, dff), BF16),
        grid=(n_steps + 1,),
        in_specs=[pl.BlockSpec((tm, d), lambda s: (cur(s) % ni, 0)),
                  pl.BlockSpec((d, tn), lambda s: (0, cur(s) // ni)),
                  pl.BlockSpec((d, tn), lambda s: (0, nj + cur(s) // ni)),
                  pl.BlockSpec((CONV_WIDTH, tn), lambda s: (0, lag(s) // ni)),
                  pl.BlockSpec((CONV_WIDTH, tn), lambda s: (0, nj + lag(s) // ni)),
                  pl.BlockSpec((1, tn), lambda s: (0, lag(s) // ni)),
                  pl.BlockSpec((1, tn), lambda s: (0, nj + lag(s) // ni))],
        out_specs=pl.BlockSpec((tm, tn), lambda s: (lag(s) % ni, lag(s) // ni)),
        scratch_shapes=[pltpu.VMEM((d, tn), BF16), pltpu.VMEM((d, tn), BF16),
                        pltpu.VMEM((2, tm + SUBLANES, tn), F32), pltpu.VMEM((2, tm + SUBLANES, tn), F32)],
        compiler_params=_params("arbitrary"),
        name="ffn_up_conv_glu",
    )(h, w_up, w_up, conv_w, conv_w, conv_b, conv_b)


def _ffn_down_kernel(a_ref, w_ref, x_ref, g_ref, o_ref):
    acc = jnp.dot(a_ref[...], w_ref[...], preferred_element_type=F32)
    o_ref[...] = x_ref[...] + g_ref[...] * acc


def _ffn_down(act, w_down_bf16, x, gate):
    t, k = act.shape
    n = w_down_bf16.shape[1]
    tm = _tile(t, 512)
    tn = _tile(n, 256)
    return pl.pallas_call(
        _ffn_down_kernel,
        out_shape=jax.ShapeDtypeStruct((t, n), F32),
        grid=(t // tm, n // tn),
        in_specs=[pl.BlockSpec((tm, k), lambda i, j: (i, 0)),
                  pl.BlockSpec((k, tn), lambda i, j: (0, j)),
                  pl.BlockSpec((tm, tn), lambda i, j: (i, j)),
                  pl.BlockSpec((1, tn), lambda i, j: (0, j))],
        out_specs=pl.BlockSpec((tm, tn), lambda i, j: (i, j)),
        compiler_params=_params("parallel", "parallel"),
        name="ffn_down",
    )(act, w_down_bf16, x, gate)


def kernel(x, c, w_ada, b_ada, g_mix_norm, w_in, b_fox_f, hgrn_lb_logits, g_fox_out, g_hgrn_out,
           w_out, g_ffn_norm, w_up, conv_w, conv_b, w_down, g_final):
    b, t, d = x.shape
    assert b == 1, "single-sequence layer"
    depth = w_ada.shape[0]
    fox_heads = b_fox_f.shape[1]
    fox_width = g_fox_out.shape[1]
    hg_kwidth = hgrn_lb_logits.shape[1]
    hg_vwidth = g_hgrn_out.shape[1]
    hg_heads = hg_vwidth // HEAD_DIM
    assert fox_width == fox_heads * HEAD_DIM and hg_kwidth == hg_heads * HEAD_DIM
    assert fox_heads <= LANES
    fox_f0 = 3 * fox_width
    hg_q0 = fox_f0 + fox_heads

    xs = x.reshape(t, d)
    row = lambda v: v.reshape(1, -1)
    for l in range(depth):
        mod = _ada_mod(c, w_ada[l], b_ada[l])
        sh1, sc1, gt1, sh2, sc2, gt2 = [mod[:, k * d:(k + 1) * d] for k in range(N_MOD)]

        w_t = jnp.swapaxes(w_in[l], 0, 1)
        b_ff = jnp.pad(b_fox_f[l], (0, LANES - fox_heads)).reshape(1, LANES)
        h1, qa, ka = _norm_mod_fox(xs, row(g_mix_norm[l]), sc1, sh1, w_t, fox_f0, b_ff, fox_heads)
        in_proj = functools.partial(_proj, [h1], w_t, transposed=True)

        qk = in_proj(0, 2 * fox_width, BF16, lead_cols=fox_width,
                     lead_scale=LOG2E * HEAD_DIM ** -0.5, name="proj_fox_qk")
        vt = in_proj(2 * fox_width, fox_width, BF16, out_transposed=True, name="proj_fox_vt")
        hqf = in_proj(hg_q0, 2 * hg_kwidth, F32, name="proj_hgrn_qf")
        hi = in_proj(hg_q0 + 2 * hg_kwidth, hg_vwidth, BF16, name="proj_hgrn_i")
        hg = in_proj(hg_q0 + 2 * hg_kwidth + hg_vwidth, hg_vwidth, F32, name="proj_hgrn_g")

        o_fox = _fox_attention(qk, qa, ka, vt, row(g_fox_out[l]), fox_heads)
        o_hg = _hgrn2(hqf, hi, hg, hgrn_lb_logits, row(g_hgrn_out[l]), hg_heads, l)

        xs = _proj([o_fox, o_hg], w_out[l], 0, d, F32, res=xs, gate=gt1, name="proj_out")

        h2 = _norm_mod(xs, row(g_ffn_norm[l]), sc2, sh2, BF16)
        act = _ffn_up(h2, w_up[l], conv_w[l], row(conv_b[l]))
        xs = _ffn_down(act, w_down[l].astype(BF16), xs, gt2)

    out = _norm_mod(xs, row(g_final), None, None, x.dtype)
    return out.reshape(b, t, d)
```

```python
import functools
import math

import numpy as np
import jax
import jax.numpy as jnp
from jax import lax
from jax.experimental import pallas as pl
from jax.experimental.pallas import tpu as pltpu

F32 = jnp.float32
BF16 = jnp.bfloat16

EPS = 1e-6
CONV_WIDTH = 3
N_MOD = 6
HEAD_DIM = 128
HGRN_CHUNK = 64
LANES = 128
SUBLANES = 8
VMEM_LIMIT = 56 * 1024 * 1024
LOG2E = 1.4426950408889634


def _params(*sem):
    return pltpu.CompilerParams(dimension_semantics=sem, vmem_limit_bytes=VMEM_LIMIT)


def _tile(n, pref):
    if n <= pref:
        return n
    t = pref
    while n % t:
        t //= 2
    return t


def _ada_kernel(cb_ref, w_ref, b_ref, o_ref, cond_ref):
    @pl.when(pl.program_id(0) == 0)
    def _():
        cb = cb_ref[...]
        cond_ref[...] = cb * jax.nn.sigmoid(cb)

    tn = o_ref.shape[1]
    for j in range(tn // LANES):
        sl = slice(j * LANES, (j + 1) * LANES)
        col = jnp.sum(w_ref[:, sl] * cond_ref[...], axis=0, keepdims=True)
        o_ref[:, sl] = col + b_ref[:, sl]


def _ada_mod(c, w_ada, b_ada):
    d, n = w_ada.shape
    cb = jnp.broadcast_to(c.reshape(d, 1), (d, LANES))
    tn = _tile(n, 512)
    return pl.pallas_call(
        _ada_kernel,
        out_shape=jax.ShapeDtypeStruct((1, n), F32),
        grid=(n // tn,),
        in_specs=[pl.BlockSpec((d, LANES), lambda j: (0, 0)),
                  pl.BlockSpec((d, tn), lambda j: (0, j)),
                  pl.BlockSpec((1, tn), lambda j: (0, j))],
        out_specs=pl.BlockSpec((1, tn), lambda j: (0, j)),
        scratch_shapes=[pltpu.VMEM((d, LANES), F32)],
        compiler_params=_params("arbitrary"),
        name="ada_mod",
    )(cb, w_ada, b_ada.reshape(1, n))


def _norm_mod_kernel(x_ref, g_ref, sc_ref, sh_ref, o_ref):
    x = x_ref[...]
    y = x * lax.rsqrt(jnp.mean(x * x, axis=-1, keepdims=True) + EPS) * g_ref[...]
    o_ref[...] = (y * (1.0 + sc_ref[...]) + sh_ref[...]).astype(o_ref.dtype)


def _norm_kernel(x_ref, g_ref, o_ref):
    x = x_ref[...]
    y = x * lax.rsqrt(jnp.mean(x * x, axis=-1, keepdims=True) + EPS) * g_ref[...]
    o_ref[...] = y.astype(o_ref.dtype)


def _norm_mod(x, g, sc, sh, out_dtype):
    t, d = x.shape
    tm = _tile(t, 256)
    row = pl.BlockSpec((tm, d), lambda i: (i, 0))
    vec = pl.BlockSpec((1, d), lambda i: (0, 0))
    if sc is None:
        kern, args, specs = _norm_kernel, (x, g), [row, vec]
    else:
        kern, args, specs = _norm_mod_kernel, (x, g, sc, sh), [row, vec, vec, vec]
    return pl.pallas_call(
        kern,
        out_shape=jax.ShapeDtypeStruct((t, d), out_dtype),
        grid=(t // tm,),
        in_specs=specs,
        out_specs=row,
        compiler_params=_params("parallel"),
        name="rms_norm",
    )(*args)


def _proj_kernel(*refs, n_a, has_res, lead_tiles, lead_scale, transposed, out_transposed):
    a_refs = refs[:n_a]
    w_ref = refs[n_a]
    pos = n_a + 1
    if has_res:
        x_ref, g_ref = refs[pos], refs[pos + 1]
        pos += 2
    o_ref, wb_ref = refs[pos], refs[pos + 1]

    @pl.when(pl.program_id(1) == 0)
    def _():
        wb_ref[...] = w_ref[...].astype(BF16)

    if out_transposed:
        o_ref[...] = lax.dot_general(wb_ref[...], a_refs[0][...], (((1,), (1,)), ((), ())),
                                     preferred_element_type=F32).astype(o_ref.dtype)
        return

    acc = None
    k0 = 0
    for a_ref in a_refs:
        ks = a_ref.shape[1]
        if transposed:
            part = lax.dot_general(a_ref[...], wb_ref[:, k0:k0 + ks], (((1,), (1,)), ((), ())),
                                   preferred_element_type=F32)
        else:
            part = jnp.dot(a_ref[...], wb_ref[k0:k0 + ks, :], preferred_element_type=F32)
        acc = part if acc is None else acc + part
        k0 += ks
    if lead_tiles:
        acc = acc * jnp.where(pl.program_id(0) < lead_tiles, lead_scale, 1.0).astype(F32)
    if has_res:
        acc = x_ref[...] + g_ref[...] * acc
    o_ref[...] = acc.astype(o_ref.dtype)


def _proj(a_list, w, col0, n, out_dtype, res=None, gate=None, lead_cols=0, lead_scale=1.0,
          transposed=False, out_transposed=False, name="proj"):
    m = a_list[0].shape[0]
    k = w.shape[1] if transposed else w.shape[0]
    assert sum(a.shape[1] for a in a_list) == k
    tm = _tile(m, 1024)
    tn = _tile(n, 512)
    assert lead_cols % tn == 0
    assert not out_transposed or (transposed and len(a_list) == 1 and res is None and not lead_cols)
    in_specs = [pl.BlockSpec((tm, a.shape[1]), lambda j, i: (i, 0)) for a in a_list]
    if transposed:
        assert col0 % SUBLANES == 0
        in_specs.append(pl.BlockSpec((pl.Element(tn), pl.Element(k)),
                                     lambda j, i: (pl.multiple_of(col0 + j * tn, SUBLANES), 0)))
        wb_shape = (tn, k)
    else:
        assert col0 % tn == 0
        in_specs.append(pl.BlockSpec((k, tn), lambda j, i: (0, col0 // tn + j)))
        wb_shape = (k, tn)
    args = list(a_list) + [w]
    if res is not None:
        in_specs += [pl.BlockSpec((tm, tn), lambda j, i: (i, j)),
                     pl.BlockSpec((1, tn), lambda j, i: (0, j))]
        args += [res, gate]
    return pl.pallas_call(
        functools.partial(_proj_kernel, n_a=len(a_list), has_res=res is not None,
                          lead_tiles=lead_cols // tn, lead_scale=lead_scale, transposed=transposed,
                          out_transposed=out_transposed),
        out_shape=jax.ShapeDtypeStruct((n, m) if out_transposed else (m, n), out_dtype),
        grid=(n // tn, m // tm),
        in_specs=in_specs,
        out_specs=(pl.BlockSpec((tn, tm), lambda j, i: (j, i)) if out_transposed
                   else pl.BlockSpec((tm, tn), lambda j, i: (i, j))),
        scratch_shapes=[pltpu.VMEM(wb_shape, BF16)],
        compiler_params=_params("parallel", "arbitrary"),
        name=name,
    )(*args)


def _split3(x):
    hi = x.astype(BF16)
    r1 = x - hi.astype(F32)
    mid = r1.astype(BF16)
    lo = (r1 - mid.astype(F32)).astype(BF16)
    return hi, mid, lo


def _bias_scatter(heads):
    w = heads * LANES
    scat = np.zeros((3 * LANES, 2 * w), np.float32)
    const = np.zeros((1, 2 * w), np.float32)
    for h in range(heads):
        for piece in range(3):
            scat[piece * LANES + h, h * LANES + piece] = 1.0
            scat[piece * LANES + h, w + h * LANES + 3 + piece] = -1.0
            const[0, h * LANES + 3 + piece] = 1.0
            const[0, w + h * LANES + piece] = 1.0
    return scat, const


def _emit_bias_terms(ff, b_ref, scat_ref, const_ref, qa_ref, ka_ref, carry_ref):
    z = ff + b_ref[...]
    logf = jnp.minimum(z, 0.0) - jnp.log1p(jnp.exp(-jnp.abs(z)))
    tt = z.shape[0]
    row = lax.broadcasted_iota(jnp.int32, (tt, tt), 0)
    col = lax.broadcasted_iota(jnp.int32, (tt, tt), 1)
    tril = jnp.where(col <= row, 1.0, 0.0).astype(BF16)
    local = None
    for piece in _split3(logf):
        part = jnp.dot(tril, piece, preferred_element_type=F32)
        local = part if local is None else local + part
    cum = local + carry_ref[0:1, :]
    carry_ref[...] = jnp.broadcast_to(cum[tt - 1:tt, :], carry_ref.shape)

    pieces = jnp.concatenate(_split3(cum * LOG2E), axis=1)
    terms = jnp.dot(pieces, scat_ref[...], preferred_element_type=F32) + const_ref[...]
    w = qa_ref.shape[1]
    qa_ref[...] = terms[:, :w].astype(BF16)
    ka_ref[...] = terms[:, w:].astype(BF16)


def _norm_fox_kernel(x_ref, g_ref, sc_ref, sh_ref, wf_ref, b_ref, scat_ref, const_ref,
                     o_ref, qa_ref, ka_ref, wfb_ref, carry_ref):
    @pl.when(pl.program_id(0) == 0)
    def _():
        wfb_ref[...] = wf_ref[...].astype(BF16)
        carry_ref[...] = jnp.zeros_like(carry_ref)

    x = x_ref[...]
    y = x * lax.rsqrt(jnp.mean(x * x, axis=-1, keepdims=True) + EPS) * g_ref[...]
    h = (y * (1.0 + sc_ref[...]) + sh_ref[...]).astype(o_ref.dtype)
    o_ref[...] = h
    ff = lax.dot_general(h, wfb_ref[...], (((1,), (1,)), ((), ())), preferred_element_type=F32)
    _emit_bias_terms(ff, b_ref, scat_ref, const_ref, qa_ref, ka_ref, carry_ref)


def _norm_mod_fox(x, g, sc, sh, w_t, f_row0, b_pad, heads):
    t, d = x.shape
    tm = _tile(t, 512)
    row = pl.BlockSpec((tm, d), lambda i: (i, 0))
    vec = pl.BlockSpec((1, d), lambda i: (0, 0))
    bias = jax.ShapeDtypeStruct((t, heads * LANES), BF16)
    bias_spec = pl.BlockSpec((tm, heads * LANES), lambda i: (i, 0))
    scat, const = _bias_scatter(heads)
    return pl.pallas_call(
        _norm_fox_kernel,
        out_shape=(jax.ShapeDtypeStruct((t, d), BF16), bias, bias),
        grid=(t // tm,),
        in_specs=[row, vec, vec, vec,
                  pl.BlockSpec((pl.Element(LANES), pl.Element(d)), lambda i: (f_row0, 0)),
                  pl.BlockSpec((1, LANES), lambda i: (0, 0)),
                  pl.BlockSpec(scat.shape, lambda i: (0, 0)),
                  pl.BlockSpec(const.shape, lambda i: (0, 0))],
        out_specs=(row, bias_spec, bias_spec),
        scratch_shapes=[pltpu.VMEM((LANES, d), BF16), pltpu.VMEM((SUBLANES, LANES), F32)],
        compiler_params=_params("arbitrary"),
        name="rms_norm_fox_bias",
    )(x, g, sc, sh, w_t, b_pad, jnp.asarray(scat, BF16), jnp.asarray(const, F32))


def _zero_from(x):
    bits = pltpu.bitcast(x, jnp.uint32)
    acc = None
    for r in range(0, bits.shape[0], SUBLANES):
        for c in range(0, bits.shape[1], LANES):
            blk = bits[r:r + SUBLANES, c:c + LANES]
            acc = blk if acc is None else acc | blk
    return (acc >> 16) >> 16


def _tie(x, zero):
    pack = 2 * SUBLANES
    z = jnp.tile(pltpu.bitcast(zero, BF16), (1, x.shape[1] // LANES))
    return jnp.concatenate([x[0:pack, :] + z, x[pack:, :]], axis=0)


V_ROWS = HEAD_DIM + 16


FOX_KEY_GROUPS = 8


def _fox_kernel(q_ref, qa_ref, k_ref, ka_ref, vt_ref, g_ref, o_ref,
                qc_ref, s_ref, mt_ref, m_ref, acc_ref):
    i = pl.program_id(1)
    tq = q_ref.shape[0]
    tk = tq
    hp = vt_ref.shape[0] // HEAD_DIM
    cols = lambda hh: slice(hh * HEAD_DIM, (hh + 1) * HEAD_DIM)
    ones = jnp.ones((V_ROWS - HEAD_DIM, tk), BF16)
    groups = FOX_KEY_GROUPS if tk % (FOX_KEY_GROUPS * 2 * SUBLANES) == 0 else 1
    gk = tk // groups

    for hh in range(hp):
        qc_ref[hh, :, 0:HEAD_DIM] = q_ref[:, cols(hh)]
        qc_ref[hh, :, HEAD_DIM:2 * HEAD_DIM] = qa_ref[:, cols(hh)]
    m_ref[...] = jnp.full_like(m_ref, -jnp.inf)
    acc_ref[...] = jnp.zeros_like(acc_ref)

    def scores(hh, j, masked, zeros=None):
        k0 = pl.multiple_of(j * tk, tk)
        kc = jnp.concatenate([k_ref[pl.ds(k0, tk), cols(hh)], ka_ref[pl.ds(k0, tk), cols(hh)]], axis=1)
        if zeros is not None:
            kc = jnp.concatenate([_tie(kc[g * gk:(g + 1) * gk], zeros[g]) for g in range(groups)], axis=0)
        s = lax.dot_general(kc, qc_ref[hh], (((1,), (1,)), ((), ())),
                            preferred_element_type=F32)
        if masked:
            key = lax.broadcasted_iota(jnp.int32, (tk, tq), 0)
            qry = lax.broadcasted_iota(jnp.int32, (tk, tq), 1)
            s = jnp.where(key <= qry, s, -jnp.inf)
        s_ref[hh] = s
        mt_ref[hh] = jnp.max(s, axis=0, keepdims=True)

    def step(j, next_masked):
        k0 = pl.multiple_of(j * tk, tk)
        for hh in range(hp):
            m_old = m_ref[hh]
            m_new = jnp.maximum(m_old, mt_ref[hh])
            alpha = jnp.exp2(m_old - m_new)
            ps = [jnp.exp2(s_ref[hh, g * gk:(g + 1) * gk, :] - m_new).astype(BF16) for g in range(groups)]
            if next_masked is not None:
                scores(hh, j + 1, next_masked, [_zero_from(pg) for pg in ps])
            vt = jnp.concatenate([vt_ref[cols(hh), pl.ds(k0, tk)], ones], axis=0)
            pv = jnp.dot(vt, jnp.concatenate(ps, axis=0), preferred_element_type=F32)
            acc_ref[hh] = alpha * acc_ref[hh] + pv
            m_ref[hh] = m_new

    @pl.when(i == 0)
    def _():
        for hh in range(hp):
            scores(hh, 0, True)

    @pl.when(i > 0)
    def _():
        for hh in range(hp):
            scores(hh, 0, False)

        def body(jj, carry):
            step(2 * jj, False)
            step(2 * jj + 1, False)
            return carry

        lax.fori_loop(0, (i - 1) // 2, body, 0)

        @pl.when((i - 1) % 2 == 1)
        def _():
            step(i - 2, False)

        step(i - 1, True)

    step(i, None)

    for hh in range(hp):
        acc = acc_ref[hh]
        o = (acc[0:HEAD_DIM, :] / acc[HEAD_DIM:HEAD_DIM + 1, :]).T
        y = o * lax.rsqrt(jnp.mean(o * o, axis=-1, keepdims=True) + EPS) * g_ref[:, cols(hh)]
        o_ref[:, cols(hh)] = y.astype(o_ref.dtype)


def _fox_attention(qk, qa, ka, vt, g_fox, heads):
    t = qk.shape[0]
    tq = _tile(t, 512)
    hp = 2 if heads % 2 == 0 else 1
    w = hp * HEAD_DIM
    groups = heads // hp
    per_group = lambda off: pl.BlockSpec((t, w), lambda h, i: (0, off + h))
    q_tile = pl.BlockSpec((tq, w), lambda h, i: (i, h))
    return pl.pallas_call(
        _fox_kernel,
        out_shape=jax.ShapeDtypeStruct((t, heads * HEAD_DIM), BF16),
        grid=(groups, t // tq),
        in_specs=[q_tile, q_tile, per_group(groups), per_group(0),
                  pl.BlockSpec((w, t), lambda h, i: (h, 0)),
                  pl.BlockSpec((1, w), lambda h, i: (0, h))],
        out_specs=q_tile,
        scratch_shapes=[pltpu.VMEM((hp, tq, 2 * HEAD_DIM), BF16),
                        pltpu.VMEM((hp, tq, tq), F32), pltpu.VMEM((hp, 1, tq), F32),
                        pltpu.VMEM((hp, 1, tq), F32), pltpu.VMEM((hp, V_ROWS, tq), F32)],
        compiler_params=_params("parallel", "arbitrary"),
        name="fox_attention",
    )(qk, qa, qk, ka, vt, g_fox)


def _hgrn_sum_matrix(c):
    levels = int(math.log2(c))
    p = np.arange(c)[:, None]
    j = np.arange(c)[None, :]
    blocks = []
    for lv in range(levels):
        h = c >> (lv + 1)
        second = ((p // h) % 2) == 1
        m_second = (p // h) * h
        m_first = (p // h + 1) * h
        blocks.append(np.where(second, (j >= m_second) & (j <= p), (j > p) & (j < m_first)))
    blocks.append(j <= p)
    blocks.append(j > p)
    return np.concatenate(blocks, axis=0).astype(np.float32)


def _hgrn_kernel(q_ref, f_ref, i_ref, gate_ref, lbl_ref, g_ref, nmat_ref, o_ref, st_ref, *, chunk, layer):
    tt = q_ref.shape[0]
    levels = int(math.log2(chunk))
    pair = 2 * chunk
    nt = lambda a, b: lax.dot_general(a, b, (((1,), (1,)), ((), ())), preferred_element_type=F32)

    @pl.when(pl.program_id(1) == 0)
    def _():
        st_ref[...] = jnp.zeros_like(st_ref)

    lbl = lbl_ref[...]
    e = jnp.exp(lbl - jnp.max(lbl, axis=0, keepdims=True))
    lb = jnp.sum(e[0:layer + 1, :], axis=0, keepdims=True) / jnp.sum(e, axis=0, keepdims=True)

    rowp = lax.broadcasted_iota(jnp.int32, (pair, HEAD_DIM), 0)
    tpos = lax.broadcasted_iota(jnp.int32, (pair, pair), 0)
    spos = lax.broadcasted_iota(jnp.int32, (pair, pair), 1)
    xor = jnp.bitwise_xor(tpos, spos)
    level_of = jnp.where((spos < tpos) & (xor < chunk), 0, -1)
    for lv in range(1, levels):
        level_of = jnp.where((spos < tpos) & (xor < (chunk >> lv)), lv, level_of)
    nmat = nmat_ref[...]

    npairs = tt // pair
    qf, kk, dec, a = {}, {}, {}, {}
    st = st_ref[...]

    def block(p, blk):
        d = dec[p][blk * chunk:(blk + 1) * chunk]
        return jnp.concatenate([d[:, :HEAD_DIM], d[:, HEAD_DIM:]], axis=0)

    def decay_stage(p):
        r = slice(p * pair, (p + 1) * pair)
        fg = lb + (1.0 - lb) * jax.nn.sigmoid(f_ref[r, :])
        logf = jnp.log(fg)
        kk[p] = 1.0 - fg
        qx = q_ref[r, :]
        qf[p] = qx * jax.nn.sigmoid(qx)
        hi = logf.astype(BF16)
        lo = (logf - hi.astype(F32)).astype(BF16)
        rhs = jnp.concatenate([jnp.concatenate([hi[:chunk], lo[:chunk]], axis=0),
                               jnp.concatenate([hi[chunk:], lo[chunk:]], axis=0)], axis=1)
        dec[p] = jnp.exp(jnp.dot(nmat, rhs, preferred_element_type=F32))

    def intra_stage(p):
        acc = jnp.where(xor == 0, nt(qf[p].astype(BF16), kk[p].astype(BF16)), 0.0)
        for lv in range(levels):
            second = jnp.bitwise_and(rowp, chunk >> (lv + 1)) != 0
            xl = (jnp.where(second, qf[p], kk[p]) * block(p, lv)).astype(BF16)
            acc = jnp.where(level_of == lv, nt(xl, xl), acc)
        a[p] = acc.astype(BF16)

    def state_stage(p, st):
        r = slice(p * pair, (p + 1) * pair)
        d_inc, d_suf = block(p, levels), block(p, levels + 1)
        qd = (qf[p] * d_inc).astype(BF16)
        kd = (kk[p] * d_suf).astype(BF16)
        ivp = i_ref[r, :]
        o_intra = jnp.dot(a[p], ivp, preferred_element_type=F32)
        o_inter = []
        for c in range(0, pair, chunk):
            upd = lax.dot_general(ivp[c:c + chunk], kd[c:c + chunk], (((0,), (0,)), ((), ())),
                                  preferred_element_type=F32)
            o_inter.append(nt(qd[c:c + chunk], st.astype(BF16)))
            st = d_inc[c + chunk - 1:c + chunk, :] * st + upd
        o = o_intra + jnp.concatenate(o_inter, axis=0)
        y = o * lax.rsqrt(jnp.mean(o * o, axis=-1, keepdims=True) + EPS) * g_ref[...]
        gx = gate_ref[r, :]
        o_ref[r, :] = (y * (gx * jax.nn.sigmoid(gx))).astype(o_ref.dtype)
        return st

    for t in range(npairs + 2):
        if t < npairs:
            decay_stage(t)
        if 1 <= t <= npairs:
            intra_stage(t - 1)
        if t >= 2:
            st = state_stage(t - 2, st)
    st_ref[...] = st


def _hgrn2(qf_proj, i_proj, g_proj, lb_logits, g_out, heads, layer):
    t = i_proj.shape[0]
    chunk = min(HGRN_CHUNK, t)
    tt = _tile(t, 16 * chunk)
    sums = _hgrn_sum_matrix(chunk)
    nmat = jnp.asarray(np.concatenate([sums, sums], axis=1), dtype=BF16)
    nl = lb_logits.shape[0]
    blk = lambda off: pl.BlockSpec((tt, HEAD_DIM), lambda h, i: (i, off + h))
    return pl.pallas_call(
        functools.partial(_hgrn_kernel, chunk=chunk, layer=layer),
        out_shape=jax.ShapeDtypeStruct((t, heads * HEAD_DIM), BF16),
        grid=(heads, t // tt),
        in_specs=[blk(0), blk(heads), blk(0), blk(0),
                  pl.BlockSpec((nl, HEAD_DIM), lambda h, i: (0, h)),
                  pl.BlockSpec((1, HEAD_DIM), lambda h, i: (0, h)),
                  pl.BlockSpec(nmat.shape, lambda h, i: (0, 0))],
        out_specs=blk(0),
        scratch_shapes=[pltpu.VMEM((HEAD_DIM, HEAD_DIM), F32)],
        compiler_params=_params("parallel", "arbitrary"),
        name="hgrn2",
    )(qf_proj, qf_proj, i_proj, g_proj, lb_logits, g_out, nmat)


MXU_DEPTH = 256


def _ffn_up_kernel(h_ref, wa_ref, wv_ref, cwa_ref, cwv_ref, cba_ref, cbv_ref, wd_ref, o_ref, wdb_ref,
                   wab_ref, wvb_ref, ua_ref, uv_ref, *, n_row_tiles, n_steps):
    tm, d = h_ref.shape
    wdb_ref[...] = wd_ref[...].astype(BF16)
    halo = SUBLANES
    slabs = d // MXU_DEPTH if d % MXU_DEPTH == 0 and tm % (d // MXU_DEPTH * SUBLANES) == 0 else 1
    sub = tm // slabs
    s = pl.program_id(0)
    row_tile = jnp.minimum(s, n_steps - 1) % n_row_tiles
    cur = s % 2
    prev = 1 - cur

    @pl.when(s == 0)
    def _():
        ua_ref[1] = jnp.zeros(ua_ref.shape[1:], F32)
        uv_ref[1] = jnp.zeros(uv_ref.shape[1:], F32)

    @pl.when(jnp.logical_and(row_tile == 0, s < n_steps))
    def _():
        wab_ref[...] = wa_ref[...].astype(BF16)
        wvb_ref[...] = wv_ref[...].astype(BF16)

    zeros = []
    for r in range(slabs):
        base = halo + r * sub

        def conv(cw_ref, cb_ref, u_ref):
            y = cb_ref[...] + cw_ref[0:1, :] * u_ref[prev, base - 2:base - 2 + sub, :]
            y = y + cw_ref[1:2, :] * u_ref[prev, base - 1:base - 1 + sub, :]
            return y + cw_ref[2:3, :] * u_ref[prev, base:base + sub, :]

        ya = conv(cwa_ref, cba_ref, ua_ref)
        yv = conv(cwv_ref, cbv_ref, uv_ref)
        act = ya * jax.nn.sigmoid(ya) * yv
        o_ref[r * sub:(r + 1) * sub, :] = act.astype(o_ref.dtype)
        zeros.append(_zero_from(act))

    first = row_tile == 0
    kw = d // slabs
    lhs = jnp.concatenate([_tie(h_ref[:, k * kw:(k + 1) * kw], zeros[k]) for k in range(slabs)], axis=1)
    for w_ref, u_ref in ((wab_ref, ua_ref), (wvb_ref, uv_ref)):
        u_ref[cur, 0:halo, :] = jnp.where(first, 0.0, u_ref[prev, tm:tm + halo, :])
        u_ref[cur, halo:halo + tm, :] = jnp.dot(lhs, w_ref[...], preferred_element_type=F32)


def _ffn_up(h, w_up, conv_w, conv_b, w_down):
    t, d = h.shape
    dff = w_up.shape[1] // 2
    tm = _tile(t, 1024)
    tn = _tile(dff, 256)
    nj = dff // tn
    ni = t // tm
    n_steps = nj * ni
    cur = lambda s: jnp.minimum(s, n_steps - 1)
    lag = lambda s: jnp.maximum(s - 1, 0)
    wd_rows = dff // n_steps
    assert dff % n_steps == 0 and wd_rows % (2 * SUBLANES) == 0
    wd_spec = pl.BlockSpec((wd_rows, w_down.shape[1]), lambda s: (cur(s), 0))
    return pl.pallas_call(
        functools.partial(_ffn_up_kernel, n_row_tiles=ni, n_steps=n_steps),
        out_shape=(jax.ShapeDtypeStruct((t, dff), BF16), jax.ShapeDtypeStruct(w_down.shape, BF16)),
        grid=(n_steps + 1,),
        in_specs=[pl.BlockSpec((tm, d), lambda s: (cur(s) % ni, 0)),
                  pl.BlockSpec((d, tn), lambda s: (0, cur(s) // ni)),
                  pl.BlockSpec((d, tn), lambda s: (0, nj + cur(s) // ni)),
                  pl.BlockSpec((CONV_WIDTH, tn), lambda s: (0, lag(s) // ni)),
                  pl.BlockSpec((CONV_WIDTH, tn), lambda s: (0, nj + lag(s) // ni)),
                  pl.BlockSpec((1, tn), lambda s: (0, lag(s) // ni)),
                  pl.BlockSpec((1, tn), lambda s: (0, nj + lag(s) // ni)),
                  wd_spec],
        out_specs=(pl.BlockSpec((tm, tn), lambda s: (lag(s) % ni, lag(s) // ni)), wd_spec),
        scratch_shapes=[pltpu.VMEM((d, tn), BF16), pltpu.VMEM((d, tn), BF16),
                        pltpu.VMEM((2, tm + SUBLANES, tn), F32), pltpu.VMEM((2, tm + SUBLANES, tn), F32)],
        compiler_params=_params("arbitrary"),
        name="ffn_up_conv_glu",
    )(h, w_up, w_up, conv_w, conv_w, conv_b, conv_b, w_down)


def _ffn_down_kernel(a_ref, w_ref, x_ref, g_ref, o_ref):
    acc = jnp.dot(a_ref[...], w_ref[...], preferred_element_type=F32)
    o_ref[...] = x_ref[...] + g_ref[...] * acc


def _ffn_down(act, w_down_bf16, x, gate):
    t, k = act.shape
    n = w_down_bf16.shape[1]
    tm = _tile(t, 512)
    tn = _tile(n, 512)
    return pl.pallas_call(
        _ffn_down_kernel,
        out_shape=jax.ShapeDtypeStruct((t, n), F32),
        grid=(t // tm, n // tn),
        in_specs=[pl.BlockSpec((tm, k), lambda i, j: (i, 0)),
                  pl.BlockSpec((k, tn), lambda i, j: (0, j)),
                  pl.BlockSpec((tm, tn), lambda i, j: (i, j)),
                  pl.BlockSpec((1, tn), lambda i, j: (0, j))],
        out_specs=pl.BlockSpec((tm, tn), lambda i, j: (i, j)),
        compiler_params=_params("parallel", "parallel"),
        name="ffn_down",
    )(act, w_down_bf16, x, gate)


def kernel(x, c, w_ada, b_ada, g_mix_norm, w_in, b_fox_f, hgrn_lb_logits, g_fox_out, g_hgrn_out,
           w_out, g_ffn_norm, w_up, conv_w, conv_b, w_down, g_final):
    b, t, d = x.shape
    assert b == 1, "single-sequence layer"
    depth = w_ada.shape[0]
    fox_heads = b_fox_f.shape[1]
    fox_width = g_fox_out.shape[1]
    hg_kwidth = hgrn_lb_logits.shape[1]
    hg_vwidth = g_hgrn_out.shape[1]
    hg_heads = hg_vwidth // HEAD_DIM
    assert fox_width == fox_heads * HEAD_DIM and hg_kwidth == hg_heads * HEAD_DIM
    assert fox_heads <= LANES
    fox_f0 = 3 * fox_width
    hg_q0 = fox_f0 + fox_heads

    xs = x.reshape(t, d)
    row = lambda v: v.reshape(1, -1)
    for l in range(depth):
        mod = _ada_mod(c, w_ada[l], b_ada[l])
        sh1, sc1, gt1, sh2, sc2, gt2 = [mod[:, k * d:(k + 1) * d] for k in range(N_MOD)]

        w_t = jnp.swapaxes(w_in[l], 0, 1)
        b_ff = jnp.pad(b_fox_f[l], (0, LANES - fox_heads)).reshape(1, LANES)
        h1, qa, ka = _norm_mod_fox(xs, row(g_mix_norm[l]), sc1, sh1, w_t, fox_f0, b_ff, fox_heads)
        in_proj = functools.partial(_proj, [h1], w_t, transposed=True)

        qk = in_proj(0, 2 * fox_width, BF16, lead_cols=fox_width,
                     lead_scale=LOG2E * HEAD_DIM ** -0.5, name="proj_fox_qk")
        vt = in_proj(2 * fox_width, fox_width, BF16, out_transposed=True, name="proj_fox_vt")
        hqf = in_proj(hg_q0, 2 * hg_kwidth, F32, name="proj_hgrn_qf")
        hi = in_proj(hg_q0 + 2 * hg_kwidth, hg_vwidth, BF16, name="proj_hgrn_i")
        hg = in_proj(hg_q0 + 2 * hg_kwidth + hg_vwidth, hg_vwidth, F32, name="proj_hgrn_g")

        o_fox = _fox_attention(qk, qa, ka, vt, row(g_fox_out[l]), fox_heads)
        o_hg = _hgrn2(hqf, hi, hg, hgrn_lb_logits, row(g_hgrn_out[l]), hg_heads, l)

        xs = _proj([o_fox, o_hg], w_out[l], 0, d, F32, res=xs, gate=gt1, name="proj_out")

        h2 = _norm_mod(xs, row(g_ffn_norm[l]), sc2, sh2, BF16)
        act, w_down_bf16 = _ffn_up(h2, w_up[l], conv_w[l], row(conv_b[l]), w_down[l])
        xs = _ffn_down(act, w_down_bf16, xs, gt2)

    out = _norm_mod(xs, row(g_final), None, None, x.dtype)
    return out.reshape(b, t, d)
```

```python
import functools
import math

import numpy as np
import jax
import jax.numpy as jnp
from jax import lax
from jax.experimental import pallas as pl
from jax.experimental.pallas import tpu as pltpu

F32 = jnp.float32
BF16 = jnp.bfloat16

EPS = 1e-6
CONV_WIDTH = 3
N_MOD = 6
HEAD_DIM = 128
HGRN_CHUNK = 64
LANES = 128
SUBLANES = 8
VMEM_LIMIT = 56 * 1024 * 1024
LOG2E = 1.4426950408889634


def _params(*sem):
    return pltpu.CompilerParams(dimension_semantics=sem, vmem_limit_bytes=VMEM_LIMIT)


def _tile(n, pref):
    if n <= pref:
        return n
    t = pref
    while n % t:
        t //= 2
    return t


def _ada_kernel(cb_ref, w_ref, b_ref, o_ref, cond_ref):
    @pl.when(pl.program_id(0) == 0)
    def _():
        cb = cb_ref[...]
        cond_ref[...] = cb * jax.nn.sigmoid(cb)

    tn = o_ref.shape[1]
    for j in range(tn // LANES):
        sl = slice(j * LANES, (j + 1) * LANES)
        col = jnp.sum(w_ref[:, sl] * cond_ref[...], axis=0, keepdims=True)
        o_ref[:, sl] = col + b_ref[:, sl]


def _ada_mod(c, w_ada, b_ada):
    d, n = w_ada.shape
    cb = jnp.broadcast_to(c.reshape(d, 1), (d, LANES))
    tn = _tile(n, 512)
    return pl.pallas_call(
        _ada_kernel,
        out_shape=jax.ShapeDtypeStruct((1, n), F32),
        grid=(n // tn,),
        in_specs=[pl.BlockSpec((d, LANES), lambda j: (0, 0)),
                  pl.BlockSpec((d, tn), lambda j: (0, j)),
                  pl.BlockSpec((1, tn), lambda j: (0, j))],
        out_specs=pl.BlockSpec((1, tn), lambda j: (0, j)),
        scratch_shapes=[pltpu.VMEM((d, LANES), F32)],
        compiler_params=_params("arbitrary"),
        name="ada_mod",
    )(cb, w_ada, b_ada.reshape(1, n))


def _norm_mod_kernel(x_ref, g_ref, sc_ref, sh_ref, o_ref):
    x = x_ref[...]
    y = x * lax.rsqrt(jnp.mean(x * x, axis=-1, keepdims=True) + EPS) * g_ref[...]
    o_ref[...] = (y * (1.0 + sc_ref[...]) + sh_ref[...]).astype(o_ref.dtype)


def _norm_kernel(x_ref, g_ref, o_ref):
    x = x_ref[...]
    y = x * lax.rsqrt(jnp.mean(x * x, axis=-1, keepdims=True) + EPS) * g_ref[...]
    o_ref[...] = y.astype(o_ref.dtype)


def _norm_mod(x, g, sc, sh, out_dtype):
    t, d = x.shape
    tm = _tile(t, 256)
    row = pl.BlockSpec((tm, d), lambda i: (i, 0))
    vec = pl.BlockSpec((1, d), lambda i: (0, 0))
    if sc is None:
        kern, args, specs = _norm_kernel, (x, g), [row, vec]
    else:
        kern, args, specs = _norm_mod_kernel, (x, g, sc, sh), [row, vec, vec, vec]
    return pl.pallas_call(
        kern,
        out_shape=jax.ShapeDtypeStruct((t, d), out_dtype),
        grid=(t // tm,),
        in_specs=specs,
        out_specs=row,
        compiler_params=_params("parallel"),
        name="rms_norm",
    )(*args)


def _proj_kernel(*refs, n_a, has_res, lead_tiles, lead_scale, transposed, out_transposed):
    a_refs = refs[:n_a]
    w_ref = refs[n_a]
    pos = n_a + 1
    if has_res:
        x_ref, g_ref = refs[pos], refs[pos + 1]
        pos += 2
    o_ref, wb_ref = refs[pos], refs[pos + 1]

    @pl.when(pl.program_id(1) == 0)
    def _():
        wb_ref[...] = w_ref[...].astype(BF16)

    if out_transposed:
        o_ref[...] = lax.dot_general(wb_ref[...], a_refs[0][...], (((1,), (1,)), ((), ())),
                                     preferred_element_type=F32).astype(o_ref.dtype)
        return

    acc = None
    k0 = 0
    for a_ref in a_refs:
        ks = a_ref.shape[1]
        if transposed:
            part = lax.dot_general(a_ref[...], wb_ref[:, k0:k0 + ks], (((1,), (1,)), ((), ())),
                                   preferred_element_type=F32)
        else:
            part = jnp.dot(a_ref[...], wb_ref[k0:k0 + ks, :], preferred_element_type=F32)
        acc = part if acc is None else acc + part
        k0 += ks
    if lead_tiles:
        acc = acc * jnp.where(pl.program_id(0) < lead_tiles, lead_scale, 1.0).astype(F32)
    if has_res:
        acc = x_ref[...] + g_ref[...] * acc
    o_ref[...] = acc.astype(o_ref.dtype)


def _proj(a_list, w, col0, n, out_dtype, res=None, gate=None, lead_cols=0, lead_scale=1.0,
          transposed=False, out_transposed=False, name="proj"):
    m = a_list[0].shape[0]
    k = w.shape[1] if transposed else w.shape[0]
    assert sum(a.shape[1] for a in a_list) == k
    tm = _tile(m, 1024)
    tn = _tile(n, 512)
    assert lead_cols % tn == 0
    assert not out_transposed or (transposed and len(a_list) == 1 and res is None and not lead_cols)
    in_specs = [pl.BlockSpec((tm, a.shape[1]), lambda j, i: (i, 0)) for a in a_list]
    if transposed:
        assert col0 % SUBLANES == 0
        in_specs.append(pl.BlockSpec((pl.Element(tn), pl.Element(k)),
                                     lambda j, i: (pl.multiple_of(col0 + j * tn, SUBLANES), 0)))
        wb_shape = (tn, k)
    else:
        assert col0 % tn == 0
        in_specs.append(pl.BlockSpec((k, tn), lambda j, i: (0, col0 // tn + j)))
        wb_shape = (k, tn)
    args = list(a_list) + [w]
    if res is not None:
        in_specs += [pl.BlockSpec((tm, tn), lambda j, i: (i, j)),
                     pl.BlockSpec((1, tn), lambda j, i: (0, j))]
        args += [res, gate]
    return pl.pallas_call(
        functools.partial(_proj_kernel, n_a=len(a_list), has_res=res is not None,
                          lead_tiles=lead_cols // tn, lead_scale=lead_scale, transposed=transposed,
                          out_transposed=out_transposed),
        out_shape=jax.ShapeDtypeStruct((n, m) if out_transposed else (m, n), out_dtype),
        grid=(n // tn, m // tm),
        in_specs=in_specs,
        out_specs=(pl.BlockSpec((tn, tm), lambda j, i: (j, i)) if out_transposed
                   else pl.BlockSpec((tm, tn), lambda j, i: (i, j))),
        scratch_shapes=[pltpu.VMEM(wb_shape, BF16)],
        compiler_params=_params("parallel", "arbitrary"),
        name=name,
    )(*args)


def _split3(x):
    hi = x.astype(BF16)
    r1 = x - hi.astype(F32)
    mid = r1.astype(BF16)
    lo = (r1 - mid.astype(F32)).astype(BF16)
    return hi, mid, lo


def _bias_scatter(heads):
    w = heads * LANES
    scat = np.zeros((3 * LANES, 2 * w), np.float32)
    const = np.zeros((1, 2 * w), np.float32)
    for h in range(heads):
        for piece in range(3):
            scat[piece * LANES + h, h * LANES + piece] = 1.0
            scat[piece * LANES + h, w + h * LANES + 3 + piece] = -1.0
            const[0, h * LANES + 3 + piece] = 1.0
            const[0, w + h * LANES + piece] = 1.0
    return scat, const


def _emit_bias_terms(ff, b_ref, scat_ref, const_ref, qa_ref, ka_ref, carry_ref):
    z = ff + b_ref[...]
    logf = jnp.minimum(z, 0.0) - jnp.log1p(jnp.exp(-jnp.abs(z)))
    tt = z.shape[0]
    row = lax.broadcasted_iota(jnp.int32, (tt, tt), 0)
    col = lax.broadcasted_iota(jnp.int32, (tt, tt), 1)
    tril = jnp.where(col <= row, 1.0, 0.0).astype(BF16)
    local = None
    for piece in _split3(logf):
        part = jnp.dot(tril, piece, preferred_element_type=F32)
        local = part if local is None else local + part
    cum = local + carry_ref[0:1, :]
    carry_ref[...] = jnp.broadcast_to(cum[tt - 1:tt, :], carry_ref.shape)

    pieces = jnp.concatenate(_split3(cum * LOG2E), axis=1)
    terms = jnp.dot(pieces, scat_ref[...], preferred_element_type=F32) + const_ref[...]
    w = qa_ref.shape[1]
    qa_ref[...] = terms[:, :w].astype(BF16)
    ka_ref[...] = terms[:, w:].astype(BF16)


def _norm_fox_kernel(x_ref, g_ref, sc_ref, sh_ref, wf_ref, b_ref, scat_ref, const_ref,
                     o_ref, qa_ref, ka_ref, wfb_ref, carry_ref):
    @pl.when(pl.program_id(0) == 0)
    def _():
        wfb_ref[...] = wf_ref[...].astype(BF16)
        carry_ref[...] = jnp.zeros_like(carry_ref)

    x = x_ref[...]
    y = x * lax.rsqrt(jnp.mean(x * x, axis=-1, keepdims=True) + EPS) * g_ref[...]
    h = (y * (1.0 + sc_ref[...]) + sh_ref[...]).astype(o_ref.dtype)
    o_ref[...] = h
    ff = lax.dot_general(h, wfb_ref[...], (((1,), (1,)), ((), ())), preferred_element_type=F32)
    _emit_bias_terms(ff, b_ref, scat_ref, const_ref, qa_ref, ka_ref, carry_ref)


def _norm_mod_fox(x, g, sc, sh, w_t, f_row0, b_pad, heads):
    t, d = x.shape
    tm = _tile(t, 512)
    row = pl.BlockSpec((tm, d), lambda i: (i, 0))
    vec = pl.BlockSpec((1, d), lambda i: (0, 0))
    bias = jax.ShapeDtypeStruct((t, heads * LANES), BF16)
    bias_spec = pl.BlockSpec((tm, heads * LANES), lambda i: (i, 0))
    scat, const = _bias_scatter(heads)
    return pl.pallas_call(
        _norm_fox_kernel,
        out_shape=(jax.ShapeDtypeStruct((t, d), BF16), bias, bias),
        grid=(t // tm,),
        in_specs=[row, vec, vec, vec,
                  pl.BlockSpec((pl.Element(LANES), pl.Element(d)), lambda i: (f_row0, 0)),
                  pl.BlockSpec((1, LANES), lambda i: (0, 0)),
                  pl.BlockSpec(scat.shape, lambda i: (0, 0)),
                  pl.BlockSpec(const.shape, lambda i: (0, 0))],
        out_specs=(row, bias_spec, bias_spec),
        scratch_shapes=[pltpu.VMEM((LANES, d), BF16), pltpu.VMEM((SUBLANES, LANES), F32)],
        compiler_params=_params("arbitrary"),
        name="rms_norm_fox_bias",
    )(x, g, sc, sh, w_t, b_pad, jnp.asarray(scat, BF16), jnp.asarray(const, F32))


def _zero_from(x):
    bits = pltpu.bitcast(x, jnp.uint32)
    acc = None
    for r in range(0, bits.shape[0], SUBLANES):
        for c in range(0, bits.shape[1], LANES):
            blk = bits[r:r + SUBLANES, c:c + LANES]
            acc = blk if acc is None else acc | blk
    return (acc >> 16) >> 16


def _tie(x, zero):
    pack = 2 * SUBLANES
    z = jnp.tile(pltpu.bitcast(zero, BF16), (1, x.shape[1] // LANES))
    return jnp.concatenate([x[0:pack, :] + z, x[pack:, :]], axis=0)


V_ROWS = HEAD_DIM + 16


FOX_KEY_GROUPS = 8


def _fox_kernel(q_ref, qa_ref, k_ref, ka_ref, vt_ref, g_ref, o_ref,
                qc_ref, s_ref, mt_ref, m_ref, acc_ref):
    i = pl.program_id(1)
    tq = q_ref.shape[0]
    tk = tq
    hp = vt_ref.shape[0] // HEAD_DIM
    cols = lambda hh: slice(hh * HEAD_DIM, (hh + 1) * HEAD_DIM)
    ones = jnp.ones((V_ROWS - HEAD_DIM, tk), BF16)
    groups = FOX_KEY_GROUPS if tk % (FOX_KEY_GROUPS * 2 * SUBLANES) == 0 else 1
    gk = tk // groups

    for hh in range(hp):
        qc = jnp.concatenate([q_ref[:, cols(hh)], qa_ref[:, cols(hh)]], axis=1)
        qc_ref[hh] = qc.astype(F32).T.astype(BF16)
    m_ref[...] = jnp.full_like(m_ref, -jnp.inf)
    acc_ref[...] = jnp.zeros_like(acc_ref)

    def scores(hh, j, masked, zeros=None):
        k0 = pl.multiple_of(j * tk, tk)
        kc = jnp.concatenate([k_ref[pl.ds(k0, tk), cols(hh)], ka_ref[pl.ds(k0, tk), cols(hh)]], axis=1)
        if zeros is not None:
            kc = jnp.concatenate([_tie(kc[g * gk:(g + 1) * gk], zeros[g]) for g in range(groups)], axis=0)
        s = jnp.dot(kc, qc_ref[hh], preferred_element_type=F32)
        if masked:
            key = lax.broadcasted_iota(jnp.int32, (tk, tq), 0)
            qry = lax.broadcasted_iota(jnp.int32, (tk, tq), 1)
            s = jnp.where(key <= qry, s, -jnp.inf)
        s_ref[hh] = s
        mt_ref[hh] = jnp.max(s, axis=0, keepdims=True)

    def step(j, next_masked):
        k0 = pl.multiple_of(j * tk, tk)
        for hh in range(hp):
            m_old = m_ref[hh]
            m_new = jnp.maximum(m_old, mt_ref[hh])
            alpha = jnp.exp2(m_old - m_new)
            ps = [jnp.exp2(s_ref[hh, g * gk:(g + 1) * gk, :] - m_new).astype(BF16) for g in range(groups)]
            if next_masked is not None:
                scores(hh, j + 1, next_masked, [_zero_from(pg) for pg in ps])
            vt = jnp.concatenate([vt_ref[cols(hh), pl.ds(k0, tk)], ones], axis=0)
            pv = jnp.dot(vt, jnp.concatenate(ps, axis=0), preferred_element_type=F32)
            acc_ref[hh] = alpha * acc_ref[hh] + pv
            m_ref[hh] = m_new

    @pl.when(i == 0)
    def _():
        for hh in range(hp):
            scores(hh, 0, True)

    @pl.when(i > 0)
    def _():
        for hh in range(hp):
            scores(hh, 0, False)

        def body(jj, carry):
            step(2 * jj, False)
            step(2 * jj + 1, False)
            return carry

        lax.fori_loop(0, (i - 1) // 2, body, 0)

        @pl.when((i - 1) % 2 == 1)
        def _():
            step(i - 2, False)

        step(i - 1, True)

    step(i, None)

    for hh in range(hp):
        acc = acc_ref[hh]
        o = (acc[0:HEAD_DIM, :] / acc[HEAD_DIM:HEAD_DIM + 1, :]).T
        y = o * lax.rsqrt(jnp.mean(o * o, axis=-1, keepdims=True) + EPS) * g_ref[:, cols(hh)]
        o_ref[:, cols(hh)] = y.astype(o_ref.dtype)


def _fox_attention(qk, qa, ka, vt, g_fox, heads):
    t = qk.shape[0]
    tq = _tile(t, 512)
    hp = 2 if heads % 2 == 0 else 1
    w = hp * HEAD_DIM
    groups = heads // hp
    per_group = lambda off: pl.BlockSpec((t, w), lambda h, i: (0, off + h))
    q_tile = pl.BlockSpec((tq, w), lambda h, i: (i, h))
    return pl.pallas_call(
        _fox_kernel,
        out_shape=jax.ShapeDtypeStruct((t, heads * HEAD_DIM), BF16),
        grid=(groups, t // tq),
        in_specs=[q_tile, q_tile, per_group(groups), per_group(0),
                  pl.BlockSpec((w, t), lambda h, i: (h, 0)),
                  pl.BlockSpec((1, w), lambda h, i: (0, h))],
        out_specs=q_tile,
        scratch_shapes=[pltpu.VMEM((hp, 2 * HEAD_DIM, tq), BF16),
                        pltpu.VMEM((hp, tq, tq), F32), pltpu.VMEM((hp, 1, tq), F32),
                        pltpu.VMEM((hp, 1, tq), F32), pltpu.VMEM((hp, V_ROWS, tq), F32)],
        compiler_params=_params("parallel", "arbitrary"),
        name="fox_attention",
    )(qk, qa, qk, ka, vt, g_fox)


def _hgrn_sum_matrix(c):
    levels = int(math.log2(c))
    p = np.arange(c)[:, None]
    j = np.arange(c)[None, :]
    blocks = []
    for lv in range(levels):
        h = c >> (lv + 1)
        second = ((p // h) % 2) == 1
        m_second = (p // h) * h
        m_first = (p // h + 1) * h
        blocks.append(np.where(second, (j >= m_second) & (j <= p), (j > p) & (j < m_first)))
    blocks.append(j <= p)
    blocks.append(j > p)
    return np.concatenate(blocks, axis=0).astype(np.float32)


def _hgrn_kernel(q_ref, f_ref, i_ref, gate_ref, lbl_ref, g_ref, nmat_ref, o_ref, st_ref, *, chunk, layer):
    tt = q_ref.shape[0]
    levels = int(math.log2(chunk))
    pair = 2 * chunk
    nt = lambda a, b: lax.dot_general(a, b, (((1,), (1,)), ((), ())), preferred_element_type=F32)

    @pl.when(pl.program_id(1) == 0)
    def _():
        st_ref[...] = jnp.zeros_like(st_ref)

    lbl = lbl_ref[...]
    e = jnp.exp(lbl - jnp.max(lbl, axis=0, keepdims=True))
    lb = jnp.sum(e[0:layer + 1, :], axis=0, keepdims=True) / jnp.sum(e, axis=0, keepdims=True)

    rowp = lax.broadcasted_iota(jnp.int32, (pair, HEAD_DIM), 0)
    tpos = lax.broadcasted_iota(jnp.int32, (pair, pair), 0)
    spos = lax.broadcasted_iota(jnp.int32, (pair, pair), 1)
    xor = jnp.bitwise_xor(tpos, spos)
    level_of = jnp.where((spos < tpos) & (xor < chunk), 0, -1)
    for lv in range(1, levels):
        level_of = jnp.where((spos < tpos) & (xor < (chunk >> lv)), lv, level_of)
    nmat = nmat_ref[...]

    npairs = tt // pair
    qf, kk, dec, a = {}, {}, {}, {}
    st = st_ref[...]

    def block(p, blk):
        d = dec[p][blk * chunk:(blk + 1) * chunk]
        return jnp.concatenate([d[:, :HEAD_DIM], d[:, HEAD_DIM:]], axis=0)

    def decay_stage(p):
        r = slice(p * pair, (p + 1) * pair)
        fg = lb + (1.0 - lb) * jax.nn.sigmoid(f_ref[r, :])
        logf = jnp.log(fg)
        kk[p] = 1.0 - fg
        qx = q_ref[r, :]
        qf[p] = qx * jax.nn.sigmoid(qx)
        hi = logf.astype(BF16)
        lo = (logf - hi.astype(F32)).astype(BF16)
        rhs = jnp.concatenate([jnp.concatenate([hi[:chunk], lo[:chunk]], axis=0),
                               jnp.concatenate([hi[chunk:], lo[chunk:]], axis=0)], axis=1)
        dec[p] = jnp.exp(jnp.dot(nmat, rhs, preferred_element_type=F32))

    def intra_stage(p):
        acc = jnp.where(xor == 0, nt(qf[p].astype(BF16), kk[p].astype(BF16)), 0.0)
        for lv in range(levels):
            second = jnp.bitwise_and(rowp, chunk >> (lv + 1)) != 0
            xl = (jnp.where(second, qf[p], kk[p]) * block(p, lv)).astype(BF16)
            acc = jnp.where(level_of == lv, nt(xl, xl), acc)
        a[p] = acc.astype(BF16)

    def state_stage(p, st):
        r = slice(p * pair, (p + 1) * pair)
        d_inc, d_suf = block(p, levels), block(p, levels + 1)
        qd = (qf[p] * d_inc).astype(BF16)
        kd = (kk[p] * d_suf).astype(BF16)
        ivp = i_ref[r, :]
        o_intra = jnp.dot(a[p], ivp, preferred_element_type=F32)
        o_inter = []
        for c in range(0, pair, chunk):
            upd = lax.dot_general(ivp[c:c + chunk], kd[c:c + chunk], (((0,), (0,)), ((), ())),
                                  preferred_element_type=F32)
            o_inter.append(nt(qd[c:c + chunk], st.astype(BF16)))
            st = d_inc[c + chunk - 1:c + chunk, :] * st + upd
        o = o_intra + jnp.concatenate(o_inter, axis=0)
        y = o * lax.rsqrt(jnp.mean(o * o, axis=-1, keepdims=True) + EPS) * g_ref[...]
        gx = gate_ref[r, :]
        o_ref[r, :] = (y * (gx * jax.nn.sigmoid(gx))).astype(o_ref.dtype)
        return st

    for t in range(npairs + 2):
        if t < npairs:
            decay_stage(t)
        if 1 <= t <= npairs:
            intra_stage(t - 1)
        if t >= 2:
            st = state_stage(t - 2, st)
    st_ref[...] = st


def _hgrn2(qf_proj, i_proj, g_proj, lb_logits, g_out, heads, layer):
    t = i_proj.shape[0]
    chunk = min(HGRN_CHUNK, t)
    tt = _tile(t, 16 * chunk)
    sums = _hgrn_sum_matrix(chunk)
    nmat = jnp.asarray(np.concatenate([sums, sums], axis=1), dtype=BF16)
    nl = lb_logits.shape[0]
    blk = lambda off: pl.BlockSpec((tt, HEAD_DIM), lambda h, i: (i, off + h))
    return pl.pallas_call(
        functools.partial(_hgrn_kernel, chunk=chunk, layer=layer),
        out_shape=jax.ShapeDtypeStruct((t, heads * HEAD_DIM), BF16),
        grid=(heads, t // tt),
        in_specs=[blk(0), blk(heads), blk(0), blk(0),
                  pl.BlockSpec((nl, HEAD_DIM), lambda h, i: (0, h)),
                  pl.BlockSpec((1, HEAD_DIM), lambda h, i: (0, h)),
                  pl.BlockSpec(nmat.shape, lambda h, i: (0, 0))],
        out_specs=blk(0),
        scratch_shapes=[pltpu.VMEM((HEAD_DIM, HEAD_DIM), F32)],
        compiler_params=_params("parallel", "arbitrary"),
        name="hgrn2",
    )(qf_proj, qf_proj, i_proj, g_proj, lb_logits, g_out, nmat)


MXU_DEPTH = 256


def _ffn_up_kernel(h_ref, wa_ref, wv_ref, cwa_ref, cwv_ref, cba_ref, cbv_ref, wd_ref, o_ref, wdb_ref,
                   wab_ref, wvb_ref, ua_ref, uv_ref, *, n_row_tiles, n_steps):
    tm, d = h_ref.shape
    wdb_ref[...] = wd_ref[...].astype(BF16)
    halo = SUBLANES
    slabs = d // MXU_DEPTH if d % MXU_DEPTH == 0 and tm % (d // MXU_DEPTH * SUBLANES) == 0 else 1
    sub = tm // slabs
    s = pl.program_id(0)
    row_tile = jnp.minimum(s, n_steps - 1) % n_row_tiles
    cur = s % 2
    prev = 1 - cur

    @pl.when(s == 0)
    def _():
        ua_ref[1] = jnp.zeros(ua_ref.shape[1:], F32)
        uv_ref[1] = jnp.zeros(uv_ref.shape[1:], F32)

    @pl.when(jnp.logical_and(row_tile == 0, s < n_steps))
    def _():
        wab_ref[...] = wa_ref[...].astype(BF16)
        wvb_ref[...] = wv_ref[...].astype(BF16)

    zeros = []
    for r in range(slabs):
        base = halo + r * sub

        def conv(cw_ref, cb_ref, u_ref):
            y = cb_ref[...] + cw_ref[0:1, :] * u_ref[prev, base - 2:base - 2 + sub, :]
            y = y + cw_ref[1:2, :] * u_ref[prev, base - 1:base - 1 + sub, :]
            return y + cw_ref[2:3, :] * u_ref[prev, base:base + sub, :]

        ya = conv(cwa_ref, cba_ref, ua_ref)
        yv = conv(cwv_ref, cbv_ref, uv_ref)
        act = ya * jax.nn.sigmoid(ya) * yv
        o_ref[r * sub:(r + 1) * sub, :] = act.astype(o_ref.dtype)
        zeros.append(_zero_from(act))

    first = row_tile == 0
    kw = d // slabs
    lhs = jnp.concatenate([_tie(h_ref[:, k * kw:(k + 1) * kw], zeros[k]) for k in range(slabs)], axis=1)
    for w_ref, u_ref in ((wab_ref, ua_ref), (wvb_ref, uv_ref)):
        u_ref[cur, 0:halo, :] = jnp.where(first, 0.0, u_ref[prev, tm:tm + halo, :])
        u_ref[cur, halo:halo + tm, :] = jnp.dot(lhs, w_ref[...], preferred_element_type=F32)


def _ffn_up(h, w_up, conv_w, conv_b, w_down):
    t, d = h.shape
    dff = w_up.shape[1] // 2
    tm = _tile(t, 1024)
    tn = _tile(dff, 256)
    nj = dff // tn
    ni = t // tm
    n_steps = nj * ni
    cur = lambda s: jnp.minimum(s, n_steps - 1)
    lag = lambda s: jnp.maximum(s - 1, 0)
    wd_rows = dff // n_steps
    assert dff % n_steps == 0 and wd_rows % (2 * SUBLANES) == 0
    wd_spec = pl.BlockSpec((wd_rows, w_down.shape[1]), lambda s: (cur(s), 0))
    return pl.pallas_call(
        functools.partial(_ffn_up_kernel, n_row_tiles=ni, n_steps=n_steps),
        out_shape=(jax.ShapeDtypeStruct((t, dff), BF16), jax.ShapeDtypeStruct(w_down.shape, BF16)),
        grid=(n_steps + 1,),
        in_specs=[pl.BlockSpec((tm, d), lambda s: (cur(s) % ni, 0)),
                  pl.BlockSpec((d, tn), lambda s: (0, cur(s) // ni)),
                  pl.BlockSpec((d, tn), lambda s: (0, nj + cur(s) // ni)),
                  pl.BlockSpec((CONV_WIDTH, tn), lambda s: (0, lag(s) // ni)),
                  pl.BlockSpec((CONV_WIDTH, tn), lambda s: (0, nj + lag(s) // ni)),
                  pl.BlockSpec((1, tn), lambda s: (0, lag(s) // ni)),
                  pl.BlockSpec((1, tn), lambda s: (0, nj + lag(s) // ni)),
                  wd_spec],
        out_specs=(pl.BlockSpec((tm, tn), lambda s: (lag(s) % ni, lag(s) // ni)), wd_spec),
        scratch_shapes=[pltpu.VMEM((d, tn), BF16), pltpu.VMEM((d, tn), BF16),
                        pltpu.VMEM((2, tm + SUBLANES, tn), F32), pltpu.VMEM((2, tm + SUBLANES, tn), F32)],
        compiler_params=_params("arbitrary"),
        name="ffn_up_conv_glu",
    )(h, w_up, w_up, conv_w, conv_w, conv_b, conv_b, w_down)


def _ffn_down_kernel(a_ref, w_ref, x_ref, g_ref, o_ref):
    acc = jnp.dot(a_ref[...], w_ref[...], preferred_element_type=F32)
    o_ref[...] = x_ref[...] + g_ref[...] * acc


def _ffn_down(act, w_down_bf16, x, gate):
    t, k = act.shape
    n = w_down_bf16.shape[1]
    tm = _tile(t, 512)
    tn = _tile(n, 512)
    return pl.pallas_call(
        _ffn_down_kernel,
        out_shape=jax.ShapeDtypeStruct((t, n), F32),
        grid=(t // tm, n // tn),
        in_specs=[pl.BlockSpec((tm, k), lambda i, j: (i, 0)),
                  pl.BlockSpec((k, tn), lambda i, j: (0, j)),
                  pl.BlockSpec((tm, tn), lambda i, j: (i, j)),
                  pl.BlockSpec((1, tn), lambda i, j: (0, j))],
        out_specs=pl.BlockSpec((tm, tn), lambda i, j: (i, j)),
        compiler_params=_params("parallel", "parallel"),
        name="ffn_down",
    )(act, w_down_bf16, x, gate)


def kernel(x, c, w_ada, b_ada, g_mix_norm, w_in, b_fox_f, hgrn_lb_logits, g_fox_out, g_hgrn_out,
           w_out, g_ffn_norm, w_up, conv_w, conv_b, w_down, g_final):
    b, t, d = x.shape
    assert b == 1, "single-sequence layer"
    depth = w_ada.shape[0]
    fox_heads = b_fox_f.shape[1]
    fox_width = g_fox_out.shape[1]
    hg_kwidth = hgrn_lb_logits.shape[1]
    hg_vwidth = g_hgrn_out.shape[1]
    hg_heads = hg_vwidth // HEAD_DIM
    assert fox_width == fox_heads * HEAD_DIM and hg_kwidth == hg_heads * HEAD_DIM
    assert fox_heads <= LANES
    fox_f0 = 3 * fox_width
    hg_q0 = fox_f0 + fox_heads

    xs = x.reshape(t, d)
    row = lambda v: v.reshape(1, -1)
    for l in range(depth):
        mod = _ada_mod(c, w_ada[l], b_ada[l])
        sh1, sc1, gt1, sh2, sc2, gt2 = [mod[:, k * d:(k + 1) * d] for k in range(N_MOD)]

        w_t = jnp.swapaxes(w_in[l], 0, 1)
        b_ff = jnp.pad(b_fox_f[l], (0, LANES - fox_heads)).reshape(1, LANES)
        h1, qa, ka = _norm_mod_fox(xs, row(g_mix_norm[l]), sc1, sh1, w_t, fox_f0, b_ff, fox_heads)
        in_proj = functools.partial(_proj, [h1], w_t, transposed=True)

        qk = in_proj(0, 2 * fox_width, BF16, lead_cols=fox_width,
                     lead_scale=LOG2E * HEAD_DIM ** -0.5, name="proj_fox_qk")
        vt = in_proj(2 * fox_width, fox_width, BF16, out_transposed=True, name="proj_fox_vt")
        hqf = in_proj(hg_q0, 2 * hg_kwidth, F32, name="proj_hgrn_qf")
        hi = in_proj(hg_q0 + 2 * hg_kwidth, hg_vwidth, BF16, name="proj_hgrn_i")
        hg = in_proj(hg_q0 + 2 * hg_kwidth + hg_vwidth, hg_vwidth, F32, name="proj_hgrn_g")

        o_fox = _fox_attention(qk, qa, ka, vt, row(g_fox_out[l]), fox_heads)
        o_hg = _hgrn2(hqf, hi, hg, hgrn_lb_logits, row(g_hgrn_out[l]), hg_heads, l)

        xs = _proj([o_fox, o_hg], w_out[l], 0, d, F32, res=xs, gate=gt1, name="proj_out")

        h2 = _norm_mod(xs, row(g_ffn_norm[l]), sc2, sh2, BF16)
        act, w_down_bf16 = _ffn_up(h2, w_up[l], conv_w[l], row(conv_b[l]), w_down[l])
        xs = _ffn_down(act, w_down_bf16, xs, gt2)

    out = _norm_mod(xs, row(g_final), None, None, x.dtype)
    return out.reshape(b, t, d)
```

```python
import functools
import math

import numpy as np
import jax
import jax.numpy as jnp
from jax import lax
from jax.experimental import pallas as pl
from jax.experimental.pallas import tpu as pltpu

F32 = jnp.float32
BF16 = jnp.bfloat16

EPS = 1e-6
CONV_WIDTH = 3
N_MOD = 6
HEAD_DIM = 128
HGRN_CHUNK = 64
LANES = 128
SUBLANES = 8
VMEM_LIMIT = 56 * 1024 * 1024
LOG2E = 1.4426950408889634


def _params(*sem):
    return pltpu.CompilerParams(dimension_semantics=sem, vmem_limit_bytes=VMEM_LIMIT)


def _tile(n, pref):
    if n <= pref:
        return n
    t = pref
    while n % t:
        t //= 2
    return t


def _ada_kernel(cb_ref, w_ref, b_ref, o_ref, cond_ref):
    @pl.when(pl.program_id(0) == 0)
    def _():
        cb = cb_ref[...]
        cond_ref[...] = cb * jax.nn.sigmoid(cb)

    tn = o_ref.shape[1]
    for j in range(tn // LANES):
        sl = slice(j * LANES, (j + 1) * LANES)
        col = jnp.sum(w_ref[:, sl] * cond_ref[...], axis=0, keepdims=True)
        o_ref[:, sl] = col + b_ref[:, sl]


def _ada_mod(c, w_ada, b_ada):
    d, n = w_ada.shape
    cb = jnp.broadcast_to(c.reshape(d, 1), (d, LANES))
    tn = _tile(n, 512)
    return pl.pallas_call(
        _ada_kernel,
        out_shape=jax.ShapeDtypeStruct((1, n), F32),
        grid=(n // tn,),
        in_specs=[pl.BlockSpec((d, LANES), lambda j: (0, 0)),
                  pl.BlockSpec((d, tn), lambda j: (0, j)),
                  pl.BlockSpec((1, tn), lambda j: (0, j))],
        out_specs=pl.BlockSpec((1, tn), lambda j: (0, j)),
        scratch_shapes=[pltpu.VMEM((d, LANES), F32)],
        compiler_params=_params("arbitrary"),
        name="ada_mod",
    )(cb, w_ada, b_ada.reshape(1, n))


def _norm_mod_kernel(x_ref, g_ref, sc_ref, sh_ref, o_ref):
    x = x_ref[...]
    y = x * lax.rsqrt(jnp.mean(x * x, axis=-1, keepdims=True) + EPS) * g_ref[...]
    o_ref[...] = (y * (1.0 + sc_ref[...]) + sh_ref[...]).astype(o_ref.dtype)


def _norm_kernel(x_ref, g_ref, o_ref):
    x = x_ref[...]
    y = x * lax.rsqrt(jnp.mean(x * x, axis=-1, keepdims=True) + EPS) * g_ref[...]
    o_ref[...] = y.astype(o_ref.dtype)


def _norm_mod(x, g, sc, sh, out_dtype):
    t, d = x.shape
    tm = _tile(t, 256)
    row = pl.BlockSpec((tm, d), lambda i: (i, 0))
    vec = pl.BlockSpec((1, d), lambda i: (0, 0))
    if sc is None:
        kern, args, specs = _norm_kernel, (x, g), [row, vec]
    else:
        kern, args, specs = _norm_mod_kernel, (x, g, sc, sh), [row, vec, vec, vec]
    return pl.pallas_call(
        kern,
        out_shape=jax.ShapeDtypeStruct((t, d), out_dtype),
        grid=(t // tm,),
        in_specs=specs,
        out_specs=row,
        compiler_params=_params("parallel"),
        name="rms_norm",
    )(*args)


def _proj_kernel(*refs, n_a, has_res, lead_tiles, lead_scale, transposed, out_transposed):
    a_refs = refs[:n_a]
    w_ref = refs[n_a]
    pos = n_a + 1
    if has_res:
        x_ref, g_ref = refs[pos], refs[pos + 1]
        pos += 2
    o_ref, wb_ref = refs[pos], refs[pos + 1]

    @pl.when(pl.program_id(1) == 0)
    def _():
        wb_ref[...] = w_ref[...].astype(BF16)

    if out_transposed:
        o_ref[...] = lax.dot_general(wb_ref[...], a_refs[0][...], (((1,), (1,)), ((), ())),
                                     preferred_element_type=F32).astype(o_ref.dtype)
        return

    acc = None
    k0 = 0
    for a_ref in a_refs:
        ks = a_ref.shape[1]
        if transposed:
            part = lax.dot_general(a_ref[...], wb_ref[:, k0:k0 + ks], (((1,), (1,)), ((), ())),
                                   preferred_element_type=F32)
        else:
            part = jnp.dot(a_ref[...], wb_ref[k0:k0 + ks, :], preferred_element_type=F32)
        acc = part if acc is None else acc + part
        k0 += ks
    if lead_tiles:
        acc = acc * jnp.where(pl.program_id(0) < lead_tiles, lead_scale, 1.0).astype(F32)
    if has_res:
        acc = x_ref[...] + g_ref[...] * acc
    o_ref[...] = acc.astype(o_ref.dtype)


def _proj(a_list, w, col0, n, out_dtype, res=None, gate=None, lead_cols=0, lead_scale=1.0,
          transposed=False, out_transposed=False, name="proj"):
    m = a_list[0].shape[0]
    k = w.shape[1] if transposed else w.shape[0]
    assert sum(a.shape[1] for a in a_list) == k
    tm = _tile(m, 1024)
    tn = _tile(n, 512)
    assert lead_cols % tn == 0
    assert not out_transposed or (transposed and len(a_list) == 1 and res is None and not lead_cols)
    in_specs = [pl.BlockSpec((tm, a.shape[1]), lambda j, i: (i, 0)) for a in a_list]
    if transposed:
        assert col0 % SUBLANES == 0
        in_specs.append(pl.BlockSpec((pl.Element(tn), pl.Element(k)),
                                     lambda j, i: (pl.multiple_of(col0 + j * tn, SUBLANES), 0)))
        wb_shape = (tn, k)
    else:
        assert col0 % tn == 0
        in_specs.append(pl.BlockSpec((k, tn), lambda j, i: (0, col0 // tn + j)))
        wb_shape = (k, tn)
    args = list(a_list) + [w]
    if res is not None:
        in_specs += [pl.BlockSpec((tm, tn), lambda j, i: (i, j)),
                     pl.BlockSpec((1, tn), lambda j, i: (0, j))]
        args += [res, gate]
    return pl.pallas_call(
        functools.partial(_proj_kernel, n_a=len(a_list), has_res=res is not None,
                          lead_tiles=lead_cols // tn, lead_scale=lead_scale, transposed=transposed,
                          out_transposed=out_transposed),
        out_shape=jax.ShapeDtypeStruct((n, m) if out_transposed else (m, n), out_dtype),
        grid=(n // tn, m // tm),
        in_specs=in_specs,
        out_specs=(pl.BlockSpec((tn, tm), lambda j, i: (j, i)) if out_transposed
                   else pl.BlockSpec((tm, tn), lambda j, i: (i, j))),
        scratch_shapes=[pltpu.VMEM(wb_shape, BF16)],
        compiler_params=_params("parallel", "arbitrary"),
        name=name,
    )(*args)


def _split3(x):
    hi = x.astype(BF16)
    r1 = x - hi.astype(F32)
    mid = r1.astype(BF16)
    lo = (r1 - mid.astype(F32)).astype(BF16)
    return hi, mid, lo


def _bias_scatter(heads):
    w = heads * LANES
    scat = np.zeros((3 * LANES, 2 * w), np.float32)
    const = np.zeros((1, 2 * w), np.float32)
    for h in range(heads):
        for piece in range(3):
            scat[piece * LANES + h, h * LANES + piece] = 1.0
            scat[piece * LANES + h, w + h * LANES + 3 + piece] = -1.0
            const[0, h * LANES + 3 + piece] = 1.0
            const[0, w + h * LANES + piece] = 1.0
    return scat, const


def _emit_bias_terms(ff, b_ref, scat_ref, const_ref, qa_ref, ka_ref, carry_ref):
    z = ff + b_ref[...]
    logf = jnp.minimum(z, 0.0) - jnp.log1p(jnp.exp(-jnp.abs(z)))
    tt = z.shape[0]
    row = lax.broadcasted_iota(jnp.int32, (tt, tt), 0)
    col = lax.broadcasted_iota(jnp.int32, (tt, tt), 1)
    tril = jnp.where(col <= row, 1.0, 0.0).astype(BF16)
    local = None
    for piece in _split3(logf):
        part = jnp.dot(tril, piece, preferred_element_type=F32)
        local = part if local is None else local + part
    cum = local + carry_ref[0:1, :]
    carry_ref[...] = jnp.broadcast_to(cum[tt - 1:tt, :], carry_ref.shape)

    pieces = jnp.concatenate(_split3(cum * LOG2E), axis=1)
    terms = jnp.dot(pieces, scat_ref[...], preferred_element_type=F32) + const_ref[...]
    w = qa_ref.shape[1]
    qa_ref[...] = terms[:, :w].astype(BF16)
    ka_ref[...] = terms[:, w:].astype(BF16)


def _norm_fox_kernel(x_ref, g_ref, sc_ref, sh_ref, wf_ref, b_ref, scat_ref, const_ref,
                     o_ref, qa_ref, ka_ref, wfb_ref, carry_ref):
    @pl.when(pl.program_id(0) == 0)
    def _():
        wfb_ref[...] = wf_ref[...].astype(BF16)
        carry_ref[...] = jnp.zeros_like(carry_ref)

    x = x_ref[...]
    y = x * lax.rsqrt(jnp.mean(x * x, axis=-1, keepdims=True) + EPS) * g_ref[...]
    h = (y * (1.0 + sc_ref[...]) + sh_ref[...]).astype(o_ref.dtype)
    o_ref[...] = h
    ff = lax.dot_general(h, wfb_ref[...], (((1,), (1,)), ((), ())), preferred_element_type=F32)
    _emit_bias_terms(ff, b_ref, scat_ref, const_ref, qa_ref, ka_ref, carry_ref)


def _norm_mod_fox(x, g, sc, sh, w_t, f_row0, b_pad, heads):
    t, d = x.shape
    tm = _tile(t, 512)
    row = pl.BlockSpec((tm, d), lambda i: (i, 0))
    vec = pl.BlockSpec((1, d), lambda i: (0, 0))
    bias = jax.ShapeDtypeStruct((t, heads * LANES), BF16)
    bias_spec = pl.BlockSpec((tm, heads * LANES), lambda i: (i, 0))
    scat, const = _bias_scatter(heads)
    return pl.pallas_call(
        _norm_fox_kernel,
        out_shape=(jax.ShapeDtypeStruct((t, d), BF16), bias, bias),
        grid=(t // tm,),
        in_specs=[row, vec, vec, vec,
                  pl.BlockSpec((pl.Element(LANES), pl.Element(d)), lambda i: (f_row0, 0)),
                  pl.BlockSpec((1, LANES), lambda i: (0, 0)),
                  pl.BlockSpec(scat.shape, lambda i: (0, 0)),
                  pl.BlockSpec(const.shape, lambda i: (0, 0))],
        out_specs=(row, bias_spec, bias_spec),
        scratch_shapes=[pltpu.VMEM((LANES, d), BF16), pltpu.VMEM((SUBLANES, LANES), F32)],
        compiler_params=_params("arbitrary"),
        name="rms_norm_fox_bias",
    )(x, g, sc, sh, w_t, b_pad, jnp.asarray(scat, BF16), jnp.asarray(const, F32))


def _zero_from(x):
    bits = pltpu.bitcast(x, jnp.uint32)
    acc = None
    for r in range(0, bits.shape[0], SUBLANES):
        for c in range(0, bits.shape[1], LANES):
            blk = bits[r:r + SUBLANES, c:c + LANES]
            acc = blk if acc is None else acc | blk
    return (acc >> 16) >> 16


def _tie(x, zero):
    pack = 2 * SUBLANES
    z = jnp.tile(pltpu.bitcast(zero, BF16), (1, x.shape[1] // LANES))
    return jnp.concatenate([x[0:pack, :] + z, x[pack:, :]], axis=0)


V_ROWS = HEAD_DIM + 16


FOX_KEY_GROUPS = 8


def _fox_kernel(q_ref, qa_ref, k_ref, ka_ref, vt_ref, g_ref, o_ref,
                qc_ref, s_ref, mt_ref, m_ref, acc_ref):
    i = pl.program_id(1)
    tq = q_ref.shape[0]
    tk = tq
    hp = vt_ref.shape[0] // HEAD_DIM
    cols = lambda hh: slice(hh * HEAD_DIM, (hh + 1) * HEAD_DIM)
    ones = jnp.ones((V_ROWS - HEAD_DIM, tk), BF16)
    groups = FOX_KEY_GROUPS if tk % (FOX_KEY_GROUPS * 2 * SUBLANES) == 0 else 1
    gk = tk // groups

    for hh in range(hp):
        qc = jnp.concatenate([q_ref[:, cols(hh)], qa_ref[:, cols(hh)]], axis=1)
        qc_ref[hh] = qc.astype(F32).T.astype(BF16)
    m_ref[...] = jnp.full_like(m_ref, -jnp.inf)
    acc_ref[...] = jnp.zeros_like(acc_ref)

    def scores(hh, j, masked, zeros=None):
        k0 = pl.multiple_of(j * tk, tk)
        kc = jnp.concatenate([k_ref[pl.ds(k0, tk), cols(hh)], ka_ref[pl.ds(k0, tk), cols(hh)]], axis=1)
        if zeros is not None:
            kc = jnp.concatenate([_tie(kc[g * gk:(g + 1) * gk], zeros[g]) for g in range(groups)], axis=0)
        s = jnp.dot(kc, qc_ref[hh], preferred_element_type=F32)
        if masked:
            key = lax.broadcasted_iota(jnp.int32, (tk, tq), 0)
            qry = lax.broadcasted_iota(jnp.int32, (tk, tq), 1)
            s = jnp.where(key <= qry, s, -jnp.inf)
        s_ref[hh] = s
        mt_ref[hh] = jnp.max(s, axis=0, keepdims=True)

    def step(j, next_masked):
        k0 = pl.multiple_of(j * tk, tk)
        for hh in range(hp):
            m_old = m_ref[hh]
            m_new = jnp.maximum(m_old, mt_ref[hh])
            alpha = jnp.exp2(m_old - m_new)
            ps = [jnp.exp2(s_ref[hh, g * gk:(g + 1) * gk, :] - m_new).astype(BF16) for g in range(groups)]
            if next_masked is not None:
                scores(hh, j + 1, next_masked, [_zero_from(pg) for pg in ps])
            vt = jnp.concatenate([vt_ref[cols(hh), pl.ds(k0, tk)], ones], axis=0)
            pv = jnp.dot(vt, jnp.concatenate(ps, axis=0), preferred_element_type=F32)
            acc_ref[hh] = alpha * acc_ref[hh] + pv
            m_ref[hh] = m_new

    @pl.when(i == 0)
    def _():
        for hh in range(hp):
            scores(hh, 0, True)

    @pl.when(i > 0)
    def _():
        for hh in range(hp):
            scores(hh, 0, False)

        def body(jj, carry):
            step(2 * jj, False)
            step(2 * jj + 1, False)
            return carry

        lax.fori_loop(0, (i - 1) // 2, body, 0)

        @pl.when((i - 1) % 2 == 1)
        def _():
            step(i - 2, False)

        step(i - 1, True)

    step(i, None)

    for hh in range(hp):
        acc = acc_ref[hh]
        o = (acc[0:HEAD_DIM, :] / acc[HEAD_DIM:HEAD_DIM + 1, :]).T
        y = o * lax.rsqrt(jnp.mean(o * o, axis=-1, keepdims=True) + EPS) * g_ref[:, cols(hh)]
        o_ref[:, cols(hh)] = y.astype(o_ref.dtype)


def _fox_attention(qk, qa, ka, vt, g_fox, heads):
    t = qk.shape[0]
    tq = _tile(t, 512)
    hp = 2 if heads % 2 == 0 else 1
    w = hp * HEAD_DIM
    groups = heads // hp
    per_group = lambda off: pl.BlockSpec((t, w), lambda h, i: (0, off + h))
    q_tile = pl.BlockSpec((tq, w), lambda h, i: (i, h))
    return pl.pallas_call(
        _fox_kernel,
        out_shape=jax.ShapeDtypeStruct((t, heads * HEAD_DIM), BF16),
        grid=(groups, t // tq),
        in_specs=[q_tile, q_tile, per_group(groups), per_group(0),
                  pl.BlockSpec((w, t), lambda h, i: (h, 0)),
                  pl.BlockSpec((1, w), lambda h, i: (0, h))],
        out_specs=q_tile,
        scratch_shapes=[pltpu.VMEM((hp, 2 * HEAD_DIM, tq), BF16),
                        pltpu.VMEM((hp, tq, tq), F32), pltpu.VMEM((hp, 1, tq), F32),
                        pltpu.VMEM((hp, 1, tq), F32), pltpu.VMEM((hp, V_ROWS, tq), F32)],
        compiler_params=_params("parallel", "arbitrary"),
        name="fox_attention",
    )(qk, qa, qk, ka, vt, g_fox)


def _hgrn_sum_matrix(c):
    levels = int(math.log2(c))
    p = np.arange(c)[:, None]
    j = np.arange(c)[None, :]
    blocks = []
    for lv in range(levels):
        h = c >> (lv + 1)
        second = ((p // h) % 2) == 1
        m_second = (p // h) * h
        m_first = (p // h + 1) * h
        blocks.append(np.where(second, (j >= m_second) & (j <= p), (j > p) & (j < m_first)))
    blocks.append(j <= p)
    blocks.append(j > p)
    return np.concatenate(blocks, axis=0).astype(np.float32)


def _hgrn_kernel(q_ref, f_ref, i_ref, gate_ref, lbl_ref, g_ref, nmat_ref, o_ref, st_ref, *, chunk, layer):
    tt = q_ref.shape[0]
    levels = int(math.log2(chunk))
    pair = 2 * chunk
    nt = lambda a, b: lax.dot_general(a, b, (((1,), (1,)), ((), ())), preferred_element_type=F32)

    @pl.when(pl.program_id(1) == 0)
    def _():
        st_ref[...] = jnp.zeros_like(st_ref)

    lbl = lbl_ref[...]
    e = jnp.exp(lbl - jnp.max(lbl, axis=0, keepdims=True))
    lb = jnp.sum(e[0:layer + 1, :], axis=0, keepdims=True) / jnp.sum(e, axis=0, keepdims=True)

    rowp = lax.broadcasted_iota(jnp.int32, (pair, HEAD_DIM), 0)
    tpos = lax.broadcasted_iota(jnp.int32, (pair, pair), 0)
    spos = lax.broadcasted_iota(jnp.int32, (pair, pair), 1)
    xor = jnp.bitwise_xor(tpos, spos)
    level_of = jnp.where((spos < tpos) & (xor < chunk), 0, -1)
    for lv in range(1, levels):
        level_of = jnp.where((spos < tpos) & (xor < (chunk >> lv)), lv, level_of)
    nmat = nmat_ref[...]

    npairs = tt // pair
    qf, kk, dec, a = {}, {}, {}, {}
    st = st_ref[...]

    def block(p, blk):
        d = dec[p][blk * chunk:(blk + 1) * chunk]
        return jnp.concatenate([d[:, :HEAD_DIM], d[:, HEAD_DIM:]], axis=0)

    def decay_stage(p):
        r = slice(p * pair, (p + 1) * pair)
        fg = lb + (1.0 - lb) * jax.nn.sigmoid(f_ref[r, :])
        logf = jnp.log(fg)
        kk[p] = 1.0 - fg
        qx = q_ref[r, :]
        qf[p] = qx * jax.nn.sigmoid(qx)
        hi = logf.astype(BF16)
        lo = (logf - hi.astype(F32)).astype(BF16)
        rhs = jnp.concatenate([jnp.concatenate([hi[:chunk], lo[:chunk]], axis=0),
                               jnp.concatenate([hi[chunk:], lo[chunk:]], axis=0)], axis=1)
        dec[p] = jnp.exp(jnp.dot(nmat, rhs, preferred_element_type=F32))

    def intra_stage(p):
        acc = jnp.where(xor == 0, nt(qf[p].astype(BF16), kk[p].astype(BF16)), 0.0)
        for lv in range(levels):
            second = jnp.bitwise_and(rowp, chunk >> (lv + 1)) != 0
            xl = (jnp.where(second, qf[p], kk[p]) * block(p, lv)).astype(BF16)
            acc = jnp.where(level_of == lv, nt(xl, xl), acc)
        a[p] = acc.astype(BF16)

    def state_stage(p, st):
        r = slice(p * pair, (p + 1) * pair)
        d_inc, d_suf = block(p, levels), block(p, levels + 1)
        qd = (qf[p] * d_inc).astype(BF16)
        kd = (kk[p] * d_suf).astype(BF16)
        ivp = i_ref[r, :]
        o_intra = jnp.dot(a[p], ivp, preferred_element_type=F32)
        o_inter = []
        for c in range(0, pair, chunk):
            upd = lax.dot_general(ivp[c:c + chunk], kd[c:c + chunk], (((0,), (0,)), ((), ())),
                                  preferred_element_type=F32)
            o_inter.append(nt(qd[c:c + chunk], st.astype(BF16)))
            st = d_inc[c + chunk - 1:c + chunk, :] * st + upd
        o = o_intra + jnp.concatenate(o_inter, axis=0)
        y = o * lax.rsqrt(jnp.mean(o * o, axis=-1, keepdims=True) + EPS) * g_ref[...]
        gx = gate_ref[r, :]
        o_ref[r, :] = (y * (gx * jax.nn.sigmoid(gx))).astype(o_ref.dtype)
        return st

    for t in range(npairs + 2):
        if t < npairs:
            decay_stage(t)
        if 1 <= t <= npairs:
            intra_stage(t - 1)
        if t >= 2:
            st = state_stage(t - 2, st)
    st_ref[...] = st


def _hgrn2(qf_proj, i_proj, g_proj, lb_logits, g_out, heads, layer):
    t = i_proj.shape[0]
    chunk = min(HGRN_CHUNK, t)
    tt = _tile(t, 32 * chunk)
    sums = _hgrn_sum_matrix(chunk)
    nmat = jnp.asarray(np.concatenate([sums, sums], axis=1), dtype=BF16)
    nl = lb_logits.shape[0]
    blk = lambda off: pl.BlockSpec((tt, HEAD_DIM), lambda h, i: (i, off + h))
    return pl.pallas_call(
        functools.partial(_hgrn_kernel, chunk=chunk, layer=layer),
        out_shape=jax.ShapeDtypeStruct((t, heads * HEAD_DIM), BF16),
        grid=(heads, t // tt),
        in_specs=[blk(0), blk(heads), blk(0), blk(0),
                  pl.BlockSpec((nl, HEAD_DIM), lambda h, i: (0, h)),
                  pl.BlockSpec((1, HEAD_DIM), lambda h, i: (0, h)),
                  pl.BlockSpec(nmat.shape, lambda h, i: (0, 0))],
        out_specs=blk(0),
        scratch_shapes=[pltpu.VMEM((HEAD_DIM, HEAD_DIM), F32)],
        compiler_params=_params("parallel", "arbitrary"),
        name="hgrn2",
    )(qf_proj, qf_proj, i_proj, g_proj, lb_logits, g_out, nmat)


MXU_DEPTH = 256


def _ffn_up_kernel(h_ref, wa_ref, wv_ref, cwa_ref, cwv_ref, cba_ref, cbv_ref, wd_ref, o_ref, wdb_ref,
                   wab_ref, wvb_ref, ua_ref, uv_ref, *, n_row_tiles, n_steps):
    tm, d = h_ref.shape
    wdb_ref[...] = wd_ref[...].astype(BF16)
    halo = SUBLANES
    slabs = d // MXU_DEPTH if d % MXU_DEPTH == 0 and tm % (d // MXU_DEPTH * SUBLANES) == 0 else 1
    sub = tm // slabs
    s = pl.program_id(0)
    row_tile = jnp.minimum(s, n_steps - 1) % n_row_tiles
    cur = s % 2
    prev = 1 - cur

    @pl.when(s == 0)
    def _():
        ua_ref[1] = jnp.zeros(ua_ref.shape[1:], F32)
        uv_ref[1] = jnp.zeros(uv_ref.shape[1:], F32)

    @pl.when(jnp.logical_and(row_tile == 0, s < n_steps))
    def _():
        wab_ref[...] = wa_ref[...].astype(BF16)
        wvb_ref[...] = wv_ref[...].astype(BF16)

    zeros = []
    for r in range(slabs):
        base = halo + r * sub

        def conv(cw_ref, cb_ref, u_ref):
            y = cb_ref[...] + cw_ref[0:1, :] * u_ref[prev, base - 2:base - 2 + sub, :]
            y = y + cw_ref[1:2, :] * u_ref[prev, base - 1:base - 1 + sub, :]
            return y + cw_ref[2:3, :] * u_ref[prev, base:base + sub, :]

        ya = conv(cwa_ref, cba_ref, ua_ref)
        yv = conv(cwv_ref, cbv_ref, uv_ref)
        act = ya * jax.nn.sigmoid(ya) * yv
        o_ref[r * sub:(r + 1) * sub, :] = act.astype(o_ref.dtype)
        zeros.append(_zero_from(act))

    first = row_tile == 0
    kw = d // slabs
    lhs = jnp.concatenate([_tie(h_ref[:, k * kw:(k + 1) * kw], zeros[k]) for k in range(slabs)], axis=1)
    for w_ref, u_ref in ((wab_ref, ua_ref), (wvb_ref, uv_ref)):
        u_ref[cur, 0:halo, :] = jnp.where(first, 0.0, u_ref[prev, tm:tm + halo, :])
        u_ref[cur, halo:halo + tm, :] = jnp.dot(lhs, w_ref[...], preferred_element_type=F32)


def _ffn_up(h, w_up, conv_w, conv_b, w_down):
    t, d = h.shape
    dff = w_up.shape[1] // 2
    tm = _tile(t, 1024)
    tn = _tile(dff, 256)
    nj = dff // tn
    ni = t // tm
    n_steps = nj * ni
    cur = lambda s: jnp.minimum(s, n_steps - 1)
    lag = lambda s: jnp.maximum(s - 1, 0)
    wd_rows = dff // n_steps
    assert dff % n_steps == 0 and wd_rows % (2 * SUBLANES) == 0
    wd_spec = pl.BlockSpec((wd_rows, w_down.shape[1]), lambda s: (cur(s), 0))
    return pl.pallas_call(
        functools.partial(_ffn_up_kernel, n_row_tiles=ni, n_steps=n_steps),
        out_shape=(jax.ShapeDtypeStruct((t, dff), BF16), jax.ShapeDtypeStruct(w_down.shape, BF16)),
        grid=(n_steps + 1,),
        in_specs=[pl.BlockSpec((tm, d), lambda s: (cur(s) % ni, 0)),
                  pl.BlockSpec((d, tn), lambda s: (0, cur(s) // ni)),
                  pl.BlockSpec((d, tn), lambda s: (0, nj + cur(s) // ni)),
                  pl.BlockSpec((CONV_WIDTH, tn), lambda s: (0, lag(s) // ni)),
                  pl.BlockSpec((CONV_WIDTH, tn), lambda s: (0, nj + lag(s) // ni)),
                  pl.BlockSpec((1, tn), lambda s: (0, lag(s) // ni)),
                  pl.BlockSpec((1, tn), lambda s: (0, nj + lag(s) // ni)),
                  wd_spec],
        out_specs=(pl.BlockSpec((tm, tn), lambda s: (lag(s) % ni, lag(s) // ni)), wd_spec),
        scratch_shapes=[pltpu.VMEM((d, tn), BF16), pltpu.VMEM((d, tn), BF16),
                        pltpu.VMEM((2, tm + SUBLANES, tn), F32), pltpu.VMEM((2, tm + SUBLANES, tn), F32)],
        compiler_params=_params("arbitrary"),
        name="ffn_up_conv_glu",
    )(h, w_up, w_up, conv_w, conv_w, conv_b, conv_b, w_down)


def _ffn_down_kernel(a_ref, w_ref, x_ref, g_ref, o_ref):
    acc = jnp.dot(a_ref[...], w_ref[...], preferred_element_type=F32)
    o_ref[...] = x_ref[...] + g_ref[...] * acc


def _ffn_down(act, w_down_bf16, x, gate):
    t, k = act.shape
    n = w_down_bf16.shape[1]
    tm = _tile(t, 512)
    tn = _tile(n, 512)
    return pl.pallas_call(
        _ffn_down_kernel,
        out_shape=jax.ShapeDtypeStruct((t, n), F32),
        grid=(t // tm, n // tn),
        in_specs=[pl.BlockSpec((tm, k), lambda i, j: (i, 0)),
                  pl.BlockSpec((k, tn), lambda i, j: (0, j)),
                  pl.BlockSpec((tm, tn), lambda i, j: (i, j)),
                  pl.BlockSpec((1, tn), lambda i, j: (0, j))],
        out_specs=pl.BlockSpec((tm, tn), lambda i, j: (i, j)),
        compiler_params=_params("parallel", "parallel"),
        name="ffn_down",
    )(act, w_down_bf16, x, gate)


def kernel(x, c, w_ada, b_ada, g_mix_norm, w_in, b_fox_f, hgrn_lb_logits, g_fox_out, g_hgrn_out,
           w_out, g_ffn_norm, w_up, conv_w, conv_b, w_down, g_final):
    b, t, d = x.shape
    assert b == 1, "single-sequence layer"
    depth = w_ada.shape[0]
    fox_heads = b_fox_f.shape[1]
    fox_width = g_fox_out.shape[1]
    hg_kwidth = hgrn_lb_logits.shape[1]
    hg_vwidth = g_hgrn_out.shape[1]
    hg_heads = hg_vwidth // HEAD_DIM
    assert fox_width == fox_heads * HEAD_DIM and hg_kwidth == hg_heads * HEAD_DIM
    assert fox_heads <= LANES
    fox_f0 = 3 * fox_width
    hg_q0 = fox_f0 + fox_heads

    xs = x.reshape(t, d)
    row = lambda v: v.reshape(1, -1)
    for l in range(depth):
        mod = _ada_mod(c, w_ada[l], b_ada[l])
        sh1, sc1, gt1, sh2, sc2, gt2 = [mod[:, k * d:(k + 1) * d] for k in range(N_MOD)]

        w_t = jnp.swapaxes(w_in[l], 0, 1)
        b_ff = jnp.pad(b_fox_f[l], (0, LANES - fox_heads)).reshape(1, LANES)
        h1, qa, ka = _norm_mod_fox(xs, row(g_mix_norm[l]), sc1, sh1, w_t, fox_f0, b_ff, fox_heads)
        in_proj = functools.partial(_proj, [h1], w_t, transposed=True)

        qk = in_proj(0, 2 * fox_width, BF16, lead_cols=fox_width,
                     lead_scale=LOG2E * HEAD_DIM ** -0.5, name="proj_fox_qk")
        vt = in_proj(2 * fox_width, fox_width, BF16, out_transposed=True, name="proj_fox_vt")
        hqf = in_proj(hg_q0, 2 * hg_kwidth, F32, name="proj_hgrn_qf")
        hi = in_proj(hg_q0 + 2 * hg_kwidth, hg_vwidth, BF16, name="proj_hgrn_i")
        hg = in_proj(hg_q0 + 2 * hg_kwidth + hg_vwidth, hg_vwidth, F32, name="proj_hgrn_g")

        o_fox = _fox_attention(qk, qa, ka, vt, row(g_fox_out[l]), fox_heads)
        o_hg = _hgrn2(hqf, hi, hg, hgrn_lb_logits, row(g_hgrn_out[l]), hg_heads, l)

        xs = _proj([o_fox, o_hg], w_out[l], 0, d, F32, res=xs, gate=gt1, name="proj_out")

        h2 = _norm_mod(xs, row(g_ffn_norm[l]), sc2, sh2, BF16)
        act, w_down_bf16 = _ffn_up(h2, w_up[l], conv_w[l], row(conv_b[l]), w_down[l])
        xs = _ffn_down(act, w_down_bf16, xs, gt2)

    out = _norm_mod(xs, row(g_final), None, None, x.dtype)
    return out.reshape(b, t, d)
```

```python
import functools
import math

import numpy as np
import jax
import jax.numpy as jnp
from jax import lax
from jax.experimental import pallas as pl
from jax.experimental.pallas import tpu as pltpu

F32 = jnp.float32
BF16 = jnp.bfloat16

EPS = 1e-6
CONV_WIDTH = 3
N_MOD = 6
HEAD_DIM = 128
HGRN_CHUNK = 64
LANES = 128
SUBLANES = 8
VMEM_LIMIT = 56 * 1024 * 1024
LOG2E = 1.4426950408889634


def _params(*sem):
    return pltpu.CompilerParams(dimension_semantics=sem, vmem_limit_bytes=VMEM_LIMIT)


def _tile(n, pref):
    if n <= pref:
        return n
    t = pref
    while n % t:
        t //= 2
    return t


def _ada_kernel(cb_ref, w_ref, b_ref, o_ref, cond_ref):
    @pl.when(pl.program_id(0) == 0)
    def _():
        cb = cb_ref[...]
        cond_ref[...] = cb * jax.nn.sigmoid(cb)

    tn = o_ref.shape[1]
    for j in range(tn // LANES):
        sl = slice(j * LANES, (j + 1) * LANES)
        col = jnp.sum(w_ref[:, sl] * cond_ref[...], axis=0, keepdims=True)
        o_ref[:, sl] = col + b_ref[:, sl]


def _ada_mod(c, w_ada, b_ada):
    d, n = w_ada.shape
    cb = jnp.broadcast_to(c.reshape(d, 1), (d, LANES))
    tn = _tile(n, 512)
    return pl.pallas_call(
        _ada_kernel,
        out_shape=jax.ShapeDtypeStruct((1, n), F32),
        grid=(n // tn,),
        in_specs=[pl.BlockSpec((d, LANES), lambda j: (0, 0)),
                  pl.BlockSpec((d, tn), lambda j: (0, j)),
                  pl.BlockSpec((1, tn), lambda j: (0, j))],
        out_specs=pl.BlockSpec((1, tn), lambda j: (0, j)),
        scratch_shapes=[pltpu.VMEM((d, LANES), F32)],
        compiler_params=_params("arbitrary"),
        name="ada_mod",
    )(cb, w_ada, b_ada.reshape(1, n))


def _norm_mod_kernel(x_ref, g_ref, sc_ref, sh_ref, o_ref):
    x = x_ref[...]
    y = x * lax.rsqrt(jnp.mean(x * x, axis=-1, keepdims=True) + EPS) * g_ref[...]
    o_ref[...] = (y * (1.0 + sc_ref[...]) + sh_ref[...]).astype(o_ref.dtype)


def _norm_kernel(x_ref, g_ref, o_ref):
    x = x_ref[...]
    y = x * lax.rsqrt(jnp.mean(x * x, axis=-1, keepdims=True) + EPS) * g_ref[...]
    o_ref[...] = y.astype(o_ref.dtype)


def _norm_mod(x, g, sc, sh, out_dtype):
    t, d = x.shape
    tm = _tile(t, 256)
    row = pl.BlockSpec((tm, d), lambda i: (i, 0))
    vec = pl.BlockSpec((1, d), lambda i: (0, 0))
    if sc is None:
        kern, args, specs = _norm_kernel, (x, g), [row, vec]
    else:
        kern, args, specs = _norm_mod_kernel, (x, g, sc, sh), [row, vec, vec, vec]
    return pl.pallas_call(
        kern,
        out_shape=jax.ShapeDtypeStruct((t, d), out_dtype),
        grid=(t // tm,),
        in_specs=specs,
        out_specs=row,
        compiler_params=_params("parallel"),
        name="rms_norm",
    )(*args)


def _proj_kernel(*refs, n_a, has_res, lead_tiles, lead_scale, transposed, out_transposed):
    a_refs = refs[:n_a]
    w_ref = refs[n_a]
    pos = n_a + 1
    if has_res:
        x_ref, g_ref = refs[pos], refs[pos + 1]
        pos += 2
    o_ref, wb_ref = refs[pos], refs[pos + 1]

    @pl.when(pl.program_id(1) == 0)
    def _():
        wb_ref[...] = w_ref[...].astype(BF16)

    if out_transposed:
        acc = lax.dot_general(wb_ref[...], a_refs[0][...], (((1,), (1,)), ((), ())),
                              preferred_element_type=F32)
        if lead_tiles:
            acc = acc * jnp.where(pl.program_id(0) < lead_tiles, lead_scale, 1.0).astype(F32)
        o_ref[...] = acc.astype(o_ref.dtype)
        return

    acc = None
    k0 = 0
    for a_ref in a_refs:
        ks = a_ref.shape[1]
        if transposed:
            part = lax.dot_general(a_ref[...], wb_ref[:, k0:k0 + ks], (((1,), (1,)), ((), ())),
                                   preferred_element_type=F32)
        else:
            part = jnp.dot(a_ref[...], wb_ref[k0:k0 + ks, :], preferred_element_type=F32)
        acc = part if acc is None else acc + part
        k0 += ks
    if lead_tiles:
        acc = acc * jnp.where(pl.program_id(0) < lead_tiles, lead_scale, 1.0).astype(F32)
    if has_res:
        acc = x_ref[...] + g_ref[...] * acc
    o_ref[...] = acc.astype(o_ref.dtype)


def _proj(a_list, w, col0, n, out_dtype, res=None, gate=None, lead_cols=0, lead_scale=1.0,
          transposed=False, out_transposed=False, name="proj"):
    m = a_list[0].shape[0]
    k = w.shape[1] if transposed else w.shape[0]
    assert sum(a.shape[1] for a in a_list) == k
    tm = _tile(m, 1024)
    tn = _tile(n, 512)
    assert lead_cols % tn == 0
    assert not out_transposed or (transposed and len(a_list) == 1 and res is None)
    in_specs = [pl.BlockSpec((tm, a.shape[1]), lambda j, i: (i, 0)) for a in a_list]
    if transposed:
        assert col0 % SUBLANES == 0
        in_specs.append(pl.BlockSpec((pl.Element(tn), pl.Element(k)),
                                     lambda j, i: (pl.multiple_of(col0 + j * tn, SUBLANES), 0)))
        wb_shape = (tn, k)
    else:
        assert col0 % tn == 0
        in_specs.append(pl.BlockSpec((k, tn), lambda j, i: (0, col0 // tn + j)))
        wb_shape = (k, tn)
    args = list(a_list) + [w]
    if res is not None:
        in_specs += [pl.BlockSpec((tm, tn), lambda j, i: (i, j)),
                     pl.BlockSpec((1, tn), lambda j, i: (0, j))]
        args += [res, gate]
    return pl.pallas_call(
        functools.partial(_proj_kernel, n_a=len(a_list), has_res=res is not None,
                          lead_tiles=lead_cols // tn, lead_scale=lead_scale, transposed=transposed,
                          out_transposed=out_transposed),
        out_shape=jax.ShapeDtypeStruct((n, m) if out_transposed else (m, n), out_dtype),
        grid=(n // tn, m // tm),
        in_specs=in_specs,
        out_specs=(pl.BlockSpec((tn, tm), lambda j, i: (j, i)) if out_transposed
                   else pl.BlockSpec((tm, tn), lambda j, i: (i, j))),
        scratch_shapes=[pltpu.VMEM(wb_shape, BF16)],
        compiler_params=_params("parallel", "arbitrary"),
        name=name,
    )(*args)


def _split3(x):
    hi = x.astype(BF16)
    r1 = x - hi.astype(F32)
    mid = r1.astype(BF16)
    lo = (r1 - mid.astype(F32)).astype(BF16)
    return hi, mid, lo


QA_ROWS = 8


def _bias_scatter(heads):
    w = heads * LANES
    scat_k = np.zeros((3 * LANES, w), np.float32)
    const_k = np.zeros((1, w), np.float32)
    scat_q = np.zeros((heads * QA_ROWS, 3 * LANES), np.float32)
    const_q = np.zeros((heads * QA_ROWS, 1), np.float32)
    for h in range(heads):
        for piece in range(3):
            scat_k[piece * LANES + h, h * LANES + 3 + piece] = -1.0
            const_k[0, h * LANES + piece] = 1.0
            scat_q[h * QA_ROWS + piece, piece * LANES + h] = 1.0
            const_q[h * QA_ROWS + 3 + piece, 0] = 1.0
    return scat_k, const_k, scat_q, const_q


def _emit_bias_terms(ff, b_ref, scat_ref, const_ref, scatq_ref, constq_ref, qat_ref, ka_ref, carry_ref):
    z = ff + b_ref[...]
    logf = jnp.minimum(z, 0.0) - jnp.log1p(jnp.exp(-jnp.abs(z)))
    tt = z.shape[0]
    row = lax.broadcasted_iota(jnp.int32, (tt, tt), 0)
    col = lax.broadcasted_iota(jnp.int32, (tt, tt), 1)
    tril = jnp.where(col <= row, 1.0, 0.0).astype(BF16)
    local = None
    for piece in _split3(logf):
        part = jnp.dot(tril, piece, preferred_element_type=F32)
        local = part if local is None else local + part
    cum = local + carry_ref[0:1, :]
    carry_ref[...] = jnp.broadcast_to(cum[tt - 1:tt, :], carry_ref.shape)

    pieces = jnp.concatenate(_split3(cum * LOG2E), axis=1)
    ka_ref[...] = (jnp.dot(pieces, scat_ref[...], preferred_element_type=F32) + const_ref[...]).astype(BF16)
    qat_ref[...] = lax.dot_general(scatq_ref[...], pieces, (((1,), (1,)), ((), ())),
                                   preferred_element_type=F32) + constq_ref[...]


def _norm_fox_kernel(x_ref, g_ref, sc_ref, sh_ref, wf_ref, b_ref, scat_ref, const_ref, scatq_ref, constq_ref,
                     o_ref, qat_ref, ka_ref, wfb_ref, carry_ref):
    @pl.when(pl.program_id(0) == 0)
    def _():
        wfb_ref[...] = wf_ref[...].astype(BF16)
        carry_ref[...] = jnp.zeros_like(carry_ref)

    x = x_ref[...]
    y = x * lax.rsqrt(jnp.mean(x * x, axis=-1, keepdims=True) + EPS) * g_ref[...]
    h = (y * (1.0 + sc_ref[...]) + sh_ref[...]).astype(o_ref.dtype)
    o_ref[...] = h
    ff = lax.dot_general(h, wfb_ref[...], (((1,), (1,)), ((), ())), preferred_element_type=F32)
    _emit_bias_terms(ff, b_ref, scat_ref, const_ref, scatq_ref, constq_ref, qat_ref, ka_ref, carry_ref)


def _norm_mod_fox(x, g, sc, sh, w_t, f_row0, b_pad, heads):
    t, d = x.shape
    tm = _tile(t, 512)
    row = pl.BlockSpec((tm, d), lambda i: (i, 0))
    vec = pl.BlockSpec((1, d), lambda i: (0, 0))
    full = lambda a: pl.BlockSpec(a.shape, lambda i: (0, 0))
    scat_k, const_k, scat_q, const_q = _bias_scatter(heads)
    return pl.pallas_call(
        _norm_fox_kernel,
        out_shape=(jax.ShapeDtypeStruct((t, d), BF16),
                   jax.ShapeDtypeStruct((heads * QA_ROWS, t), F32),
                   jax.ShapeDtypeStruct((t, heads * LANES), BF16)),
        grid=(t // tm,),
        in_specs=[row, vec, vec, vec,
                  pl.BlockSpec((pl.Element(LANES), pl.Element(d)), lambda i: (f_row0, 0)),
                  pl.BlockSpec((1, LANES), lambda i: (0, 0)),
                  full(scat_k), full(const_k), full(scat_q), full(const_q)],
        out_specs=(row, pl.BlockSpec((heads * QA_ROWS, tm), lambda i: (0, i)),
                   pl.BlockSpec((tm, heads * LANES), lambda i: (i, 0))),
        scratch_shapes=[pltpu.VMEM((LANES, d), BF16), pltpu.VMEM((SUBLANES, LANES), F32)],
        compiler_params=_params("arbitrary"),
        name="rms_norm_fox_bias",
    )(x, g, sc, sh, w_t, b_pad, jnp.asarray(scat_k, BF16), jnp.asarray(const_k, F32),
      jnp.asarray(scat_q, BF16), jnp.asarray(const_q, F32))


def _zero_from(x):
    bits = pltpu.bitcast(x, jnp.uint32)
    acc = None
    for r in range(0, bits.shape[0], SUBLANES):
        for c in range(0, bits.shape[1], LANES):
            blk = bits[r:r + SUBLANES, c:c + LANES]
            acc = blk if acc is None else acc | blk
    return (acc >> 16) >> 16


def _tie(x, zero):
    pack = 2 * SUBLANES
    z = jnp.tile(pltpu.bitcast(zero, BF16), (1, x.shape[1] // LANES))
    return jnp.concatenate([x[0:pack, :] + z, x[pack:, :]], axis=0)


V_ROWS = HEAD_DIM + 16


FOX_KEY_GROUPS = 8


def _fox_kernel(qt_ref, qat_ref, k_ref, ka_ref, vt_ref, g_ref, o_ref,
                qc_ref, s_ref, mt_ref, m_ref, acc_ref):
    i = pl.program_id(1)
    tq = qt_ref.shape[1]
    tk = tq
    hp = vt_ref.shape[0] // HEAD_DIM
    cols = lambda hh: slice(hh * HEAD_DIM, (hh + 1) * HEAD_DIM)
    ones = jnp.ones((V_ROWS - HEAD_DIM, tk), BF16)
    groups = FOX_KEY_GROUPS if tk % (FOX_KEY_GROUPS * 2 * SUBLANES) == 0 else 1
    gk = tk // groups

    for hh in range(hp):
        qc_ref[hh, 0:HEAD_DIM, :] = qt_ref[cols(hh), :]
        bias_rows = jnp.concatenate([qat_ref[hh * QA_ROWS:(hh + 1) * QA_ROWS, :],
                                     jnp.zeros((HEAD_DIM - QA_ROWS, tq), F32)], axis=0)
        qc_ref[hh, HEAD_DIM:2 * HEAD_DIM, :] = bias_rows.astype(BF16)
    m_ref[...] = jnp.full_like(m_ref, -jnp.inf)
    acc_ref[...] = jnp.zeros_like(acc_ref)

    def scores(hh, j, masked, zeros=None):
        k0 = pl.multiple_of(j * tk, tk)
        kc = jnp.concatenate([k_ref[pl.ds(k0, tk), cols(hh)], ka_ref[pl.ds(k0, tk), cols(hh)]], axis=1)
        if zeros is not None:
            kc = jnp.concatenate([_tie(kc[g * gk:(g + 1) * gk], zeros[g]) for g in range(groups)], axis=0)
        s = jnp.dot(kc, qc_ref[hh], preferred_element_type=F32)
        if masked:
            key = lax.broadcasted_iota(jnp.int32, (tk, tq), 0)
            qry = lax.broadcasted_iota(jnp.int32, (tk, tq), 1)
            s = jnp.where(key <= qry, s, -jnp.inf)
        s_ref[hh] = s
        mt_ref[hh] = jnp.max(s, axis=0, keepdims=True)

    def step(j, next_masked):
        k0 = pl.multiple_of(j * tk, tk)
        for hh in range(hp):
            m_old = m_ref[hh]
            m_new = jnp.maximum(m_old, mt_ref[hh])
            alpha = jnp.exp2(m_old - m_new)
            ps = [jnp.exp2(s_ref[hh, g * gk:(g + 1) * gk, :] - m_new).astype(BF16) for g in range(groups)]
            if next_masked is not None:
                scores(hh, j + 1, next_masked, [_zero_from(pg) for pg in ps])
            vt = jnp.concatenate([vt_ref[cols(hh), pl.ds(k0, tk)], ones], axis=0)
            pv = jnp.dot(vt, jnp.concatenate(ps, axis=0), preferred_element_type=F32)
            acc_ref[hh] = alpha * acc_ref[hh] + pv
            m_ref[hh] = m_new

    @pl.when(i == 0)
    def _():
        for hh in range(hp):
            scores(hh, 0, True)

    @pl.when(i > 0)
    def _():
        for hh in range(hp):
            scores(hh, 0, False)

        def body(jj, carry):
            step(2 * jj, False)
            step(2 * jj + 1, False)
            return carry

        lax.fori_loop(0, (i - 1) // 2, body, 0)

        @pl.when((i - 1) % 2 == 1)
        def _():
            step(i - 2, False)

        step(i - 1, True)

    step(i, None)

    for hh in range(hp):
        acc = acc_ref[hh]
        o = (acc[0:HEAD_DIM, :] / acc[HEAD_DIM:HEAD_DIM + 1, :]).T
        y = o * lax.rsqrt(jnp.mean(o * o, axis=-1, keepdims=True) + EPS) * g_ref[:, cols(hh)]
        o_ref[:, cols(hh)] = y.astype(o_ref.dtype)


def _fox_attention(qt, qat, k, ka, vt, g_fox, heads):
    t = k.shape[0]
    tq = _tile(t, 512)
    hp = 2 if heads % 2 == 0 else 1
    w = hp * HEAD_DIM
    per_group = pl.BlockSpec((t, w), lambda h, i: (0, h))
    return pl.pallas_call(
        _fox_kernel,
        out_shape=jax.ShapeDtypeStruct((t, heads * HEAD_DIM), BF16),
        grid=(heads // hp, t // tq),
        in_specs=[pl.BlockSpec((w, tq), lambda h, i: (h, i)),
                  pl.BlockSpec((hp * QA_ROWS, tq), lambda h, i: (h, i)),
                  per_group, per_group,
                  pl.BlockSpec((w, t), lambda h, i: (h, 0)),
                  pl.BlockSpec((1, w), lambda h, i: (0, h))],
        out_specs=pl.BlockSpec((tq, w), lambda h, i: (i, h)),
        scratch_shapes=[pltpu.VMEM((hp, 2 * HEAD_DIM, tq), BF16),
                        pltpu.VMEM((hp, tq, tq), F32), pltpu.VMEM((hp, 1, tq), F32),
                        pltpu.VMEM((hp, 1, tq), F32), pltpu.VMEM((hp, V_ROWS, tq), F32)],
        compiler_params=_params("parallel", "arbitrary"),
        name="fox_attention",
    )(qt, qat, k, ka, vt, g_fox)


def _hgrn_sum_matrix(c):
    levels = int(math.log2(c))
    p = np.arange(c)[:, None]
    j = np.arange(c)[None, :]
    blocks = []
    for lv in range(levels):
        h = c >> (lv + 1)
        second = ((p // h) % 2) == 1
        m_second = (p // h) * h
        m_first = (p // h + 1) * h
        blocks.append(np.where(second, (j >= m_second) & (j <= p), (j > p) & (j < m_first)))
    blocks.append(j <= p)
    blocks.append(j > p)
    return np.concatenate(blocks, axis=0).astype(np.float32)


def _hgrn_kernel(q_ref, f_ref, i_ref, gate_ref, lbl_ref, g_ref, nmat_ref, o_ref, st_ref, *, chunk, layer):
    tt = q_ref.shape[0]
    levels = int(math.log2(chunk))
    pair = 2 * chunk
    nt = lambda a, b: lax.dot_general(a, b, (((1,), (1,)), ((), ())), preferred_element_type=F32)

    @pl.when(pl.program_id(1) == 0)
    def _():
        st_ref[...] = jnp.zeros_like(st_ref)

    lbl = lbl_ref[...]
    e = jnp.exp(lbl - jnp.max(lbl, axis=0, keepdims=True))
    lb = jnp.sum(e[0:layer + 1, :], axis=0, keepdims=True) / jnp.sum(e, axis=0, keepdims=True)

    rowp = lax.broadcasted_iota(jnp.int32, (pair, HEAD_DIM), 0)
    tpos = lax.broadcasted_iota(jnp.int32, (pair, pair), 0)
    spos = lax.broadcasted_iota(jnp.int32, (pair, pair), 1)
    xor = jnp.bitwise_xor(tpos, spos)
    level_of = jnp.where((spos < tpos) & (xor < chunk), 0, -1)
    for lv in range(1, levels):
        level_of = jnp.where((spos < tpos) & (xor < (chunk >> lv)), lv, level_of)
    nmat = nmat_ref[...]

    npairs = tt // pair
    qf, kk, dec, a = {}, {}, {}, {}
    st = st_ref[...]

    def block(p, blk):
        d = dec[p][blk * chunk:(blk + 1) * chunk]
        return jnp.concatenate([d[:, :HEAD_DIM], d[:, HEAD_DIM:]], axis=0)

    def decay_stage(p):
        r = slice(p * pair, (p + 1) * pair)
        fg = lb + (1.0 - lb) * jax.nn.sigmoid(f_ref[r, :])
        logf = jnp.log(fg)
        kk[p] = 1.0 - fg
        qx = q_ref[r, :]
        qf[p] = qx * jax.nn.sigmoid(qx)
        hi = logf.astype(BF16)
        lo = (logf - hi.astype(F32)).astype(BF16)
        rhs = jnp.concatenate([jnp.concatenate([hi[:chunk], lo[:chunk]], axis=0),
                               jnp.concatenate([hi[chunk:], lo[chunk:]], axis=0)], axis=1)
        dec[p] = jnp.exp(jnp.dot(nmat, rhs, preferred_element_type=F32))

    def intra_stage(p):
        acc = jnp.where(xor == 0, nt(qf[p].astype(BF16), kk[p].astype(BF16)), 0.0)
        for lv in range(levels):
            second = jnp.bitwise_and(rowp, chunk >> (lv + 1)) != 0
            xl = (jnp.where(second, qf[p], kk[p]) * block(p, lv)).astype(BF16)
            acc = jnp.where(level_of == lv, nt(xl, xl), acc)
        a[p] = acc.astype(BF16)

    def state_stage(p, st):
        r = slice(p * pair, (p + 1) * pair)
        d_inc, d_suf = block(p, levels), block(p, levels + 1)
        qd = (qf[p] * d_inc).astype(BF16)
        kd = (kk[p] * d_suf).astype(BF16)
        ivp = i_ref[r, :]
        o_intra = jnp.dot(a[p], ivp, preferred_element_type=F32)
        o_inter = []
        for c in range(0, pair, chunk):
            upd = lax.dot_general(ivp[c:c + chunk], kd[c:c + chunk], (((0,), (0,)), ((), ())),
                                  preferred_element_type=F32)
            o_inter.append(nt(qd[c:c + chunk], st.astype(BF16)))
            st = d_inc[c + chunk - 1:c + chunk, :] * st + upd
        o = o_intra + jnp.concatenate(o_inter, axis=0)
        y = o * lax.rsqrt(jnp.mean(o * o, axis=-1, keepdims=True) + EPS) * g_ref[...]
        gx = gate_ref[r, :]
        o_ref[r, :] = (y * (gx * jax.nn.sigmoid(gx))).astype(o_ref.dtype)
        return st

    for t in range(npairs + 2):
        if t < npairs:
            decay_stage(t)
        if 1 <= t <= npairs:
            intra_stage(t - 1)
        if t >= 2:
            st = state_stage(t - 2, st)
    st_ref[...] = st


def _hgrn2(qf_proj, i_proj, g_proj, lb_logits, g_out, heads, layer):
    t = i_proj.shape[0]
    chunk = min(HGRN_CHUNK, t)
    tt = _tile(t, 32 * chunk)
    sums = _hgrn_sum_matrix(chunk)
    nmat = jnp.asarray(np.concatenate([sums, sums], axis=1), dtype=BF16)
    nl = lb_logits.shape[0]
    blk = lambda off: pl.BlockSpec((tt, HEAD_DIM), lambda h, i: (i, off + h))
    return pl.pallas_call(
        functools.partial(_hgrn_kernel, chunk=chunk, layer=layer),
        out_shape=jax.ShapeDtypeStruct((t, heads * HEAD_DIM), BF16),
        grid=(heads, t // tt),
        in_specs=[blk(0), blk(heads), blk(0), blk(0),
                  pl.BlockSpec((nl, HEAD_DIM), lambda h, i: (0, h)),
                  pl.BlockSpec((1, HEAD_DIM), lambda h, i: (0, h)),
                  pl.BlockSpec(nmat.shape, lambda h, i: (0, 0))],
        out_specs=blk(0),
        scratch_shapes=[pltpu.VMEM((HEAD_DIM, HEAD_DIM), F32)],
        compiler_params=_params("parallel", "arbitrary"),
        name="hgrn2",
    )(qf_proj, qf_proj, i_proj, g_proj, lb_logits, g_out, nmat)


MXU_DEPTH = 256


def _ffn_up_kernel(h_ref, wa_ref, wv_ref, cwa_ref, cwv_ref, cba_ref, cbv_ref, wd_ref, o_ref, wdb_ref,
                   wab_ref, wvb_ref, ua_ref, uv_ref, *, n_row_tiles, n_steps):
    tm, d = h_ref.shape
    wdb_ref[...] = wd_ref[...].astype(BF16)
    halo = SUBLANES
    slabs = d // MXU_DEPTH if d % MXU_DEPTH == 0 and tm % (d // MXU_DEPTH * SUBLANES) == 0 else 1
    sub = tm // slabs
    s = pl.program_id(0)
    row_tile = jnp.minimum(s, n_steps - 1) % n_row_tiles
    cur = s % 2
    prev = 1 - cur

    @pl.when(s == 0)
    def _():
        ua_ref[1] = jnp.zeros(ua_ref.shape[1:], F32)
        uv_ref[1] = jnp.zeros(uv_ref.shape[1:], F32)

    @pl.when(jnp.logical_and(row_tile == 0, s < n_steps))
    def _():
        wab_ref[...] = wa_ref[...].astype(BF16)
        wvb_ref[...] = wv_ref[...].astype(BF16)

    zeros = []
    for r in range(slabs):
        base = halo + r * sub

        def conv(cw_ref, cb_ref, u_ref):
            y = cb_ref[...] + cw_ref[0:1, :] * u_ref[prev, base - 2:base - 2 + sub, :]
            y = y + cw_ref[1:2, :] * u_ref[prev, base - 1:base - 1 + sub, :]
            return y + cw_ref[2:3, :] * u_ref[prev, base:base + sub, :]

        ya = conv(cwa_ref, cba_ref, ua_ref)
        yv = conv(cwv_ref, cbv_ref, uv_ref)
        act = ya * jax.nn.sigmoid(ya) * yv
        o_ref[r * sub:(r + 1) * sub, :] = act.astype(o_ref.dtype)
        zeros.append(_zero_from(act))

    first = row_tile == 0
    kw = d // slabs
    lhs = jnp.concatenate([_tie(h_ref[:, k * kw:(k + 1) * kw], zeros[k]) for k in range(slabs)], axis=1)
    for w_ref, u_ref in ((wab_ref, ua_ref), (wvb_ref, uv_ref)):
        u_ref[cur, 0:halo, :] = jnp.where(first, 0.0, u_ref[prev, tm:tm + halo, :])
        u_ref[cur, halo:halo + tm, :] = jnp.dot(lhs, w_ref[...], preferred_element_type=F32)


def _ffn_up(h, w_up, conv_w, conv_b, w_down):
    t, d = h.shape
    dff = w_up.shape[1] // 2
    tm = _tile(t, 1024)
    tn = _tile(dff, 256)
    nj = dff // tn
    ni = t // tm
    n_steps = nj * ni
    cur = lambda s: jnp.minimum(s, n_steps - 1)
    lag = lambda s: jnp.maximum(s - 1, 0)
    wd_rows = dff // n_steps
    assert dff % n_steps == 0 and wd_rows % (2 * SUBLANES) == 0
    wd_spec = pl.BlockSpec((wd_rows, w_down.shape[1]), lambda s: (cur(s), 0))
    return pl.pallas_call(
        functools.partial(_ffn_up_kernel, n_row_tiles=ni, n_steps=n_steps),
        out_shape=(jax.ShapeDtypeStruct((t, dff), BF16), jax.ShapeDtypeStruct(w_down.shape, BF16)),
        grid=(n_steps + 1,),
        in_specs=[pl.BlockSpec((tm, d), lambda s: (cur(s) % ni, 0)),
                  pl.BlockSpec((d, tn), lambda s: (0, cur(s) // ni)),
                  pl.BlockSpec((d, tn), lambda s: (0, nj + cur(s) // ni)),
                  pl.BlockSpec((CONV_WIDTH, tn), lambda s: (0, lag(s) // ni)),
                  pl.BlockSpec((CONV_WIDTH, tn), lambda s: (0, nj + lag(s) // ni)),
                  pl.BlockSpec((1, tn), lambda s: (0, lag(s) // ni)),
                  pl.BlockSpec((1, tn), lambda s: (0, nj + lag(s) // ni)),
                  wd_spec],
        out_specs=(pl.BlockSpec((tm, tn), lambda s: (lag(s) % ni, lag(s) // ni)), wd_spec),
        scratch_shapes=[pltpu.VMEM((d, tn), BF16), pltpu.VMEM((d, tn), BF16),
                        pltpu.VMEM((2, tm + SUBLANES, tn), F32), pltpu.VMEM((2, tm + SUBLANES, tn), F32)],
        compiler_params=_params("arbitrary"),
        name="ffn_up_conv_glu",
    )(h, w_up, w_up, conv_w, conv_w, conv_b, conv_b, w_down)


def _ffn_down_kernel(a_ref, w_ref, x_ref, g_ref, o_ref):
    acc = jnp.dot(a_ref[...], w_ref[...], preferred_element_type=F32)
    o_ref[...] = x_ref[...] + g_ref[...] * acc


def _ffn_down(act, w_down_bf16, x, gate):
    t, k = act.shape
    n = w_down_bf16.shape[1]
    tm = _tile(t, 512)
    tn = _tile(n, 512)
    return pl.pallas_call(
        _ffn_down_kernel,
        out_shape=jax.ShapeDtypeStruct((t, n), F32),
        grid=(t // tm, n // tn),
        in_specs=[pl.BlockSpec((tm, k), lambda i, j: (i, 0)),
                  pl.BlockSpec((k, tn), lambda i, j: (0, j)),
                  pl.BlockSpec((tm, tn), lambda i, j: (i, j)),
                  pl.BlockSpec((1, tn), lambda i, j: (0, j))],
        out_specs=pl.BlockSpec((tm, tn), lambda i, j: (i, j)),
        compiler_params=_params("parallel", "parallel"),
        name="ffn_down",
    )(act, w_down_bf16, x, gate)


def kernel(x, c, w_ada, b_ada, g_mix_norm, w_in, b_fox_f, hgrn_lb_logits, g_fox_out, g_hgrn_out,
           w_out, g_ffn_norm, w_up, conv_w, conv_b, w_down, g_final):
    b, t, d = x.shape
    assert b == 1, "single-sequence layer"
    depth = w_ada.shape[0]
    fox_heads = b_fox_f.shape[1]
    fox_width = g_fox_out.shape[1]
    hg_kwidth = hgrn_lb_logits.shape[1]
    hg_vwidth = g_hgrn_out.shape[1]
    hg_heads = hg_vwidth // HEAD_DIM
    assert fox_width == fox_heads * HEAD_DIM and hg_kwidth == hg_heads * HEAD_DIM
    assert fox_heads <= LANES
    fox_f0 = 3 * fox_width
    hg_q0 = fox_f0 + fox_heads

    xs = x.reshape(t, d)
    row = lambda v: v.reshape(1, -1)
    for l in range(depth):
        mod = _ada_mod(c, w_ada[l], b_ada[l])
        sh1, sc1, gt1, sh2, sc2, gt2 = [mod[:, k * d:(k + 1) * d] for k in range(N_MOD)]

        w_t = jnp.swapaxes(w_in[l], 0, 1)
        b_ff = jnp.pad(b_fox_f[l], (0, LANES - fox_heads)).reshape(1, LANES)
        h1, qat, ka = _norm_mod_fox(xs, row(g_mix_norm[l]), sc1, sh1, w_t, fox_f0, b_ff, fox_heads)
        in_proj = functools.partial(_proj, [h1], w_t, transposed=True)

        qt = in_proj(0, fox_width, BF16, out_transposed=True, lead_cols=fox_width,
                     lead_scale=LOG2E * HEAD_DIM ** -0.5, name="proj_fox_qt")
        kx = in_proj(fox_width, fox_width, BF16, name="proj_fox_k")
        vt = in_proj(2 * fox_width, fox_width, BF16, out_transposed=True, name="proj_fox_vt")
        hqf = in_proj(hg_q0, 2 * hg_kwidth, F32, name="proj_hgrn_qf")
        hi = in_proj(hg_q0 + 2 * hg_kwidth, hg_vwidth, BF16, name="proj_hgrn_i")
        hg = in_proj(hg_q0 + 2 * hg_kwidth + hg_vwidth, hg_vwidth, F32, name="proj_hgrn_g")

        o_fox = _fox_attention(qt, qat, kx, ka, vt, row(g_fox_out[l]), fox_heads)
        o_hg = _hgrn2(hqf, hi, hg, hgrn_lb_logits, row(g_hgrn_out[l]), hg_heads, l)

        xs = _proj([o_fox, o_hg], w_out[l], 0, d, F32, res=xs, gate=gt1, name="proj_out")

        h2 = _norm_mod(xs, row(g_ffn_norm[l]), sc2, sh2, BF16)
        act, w_down_bf16 = _ffn_up(h2, w_up[l], conv_w[l], row(conv_b[l]), w_down[l])
        xs = _ffn_down(act, w_down_bf16, xs, gt2)

    out = _norm_mod(xs, row(g_final), None, None, x.dtype)
    return out.reshape(b, t, d)
```

```python
import functools
import math

import numpy as np
import jax
import jax.numpy as jnp
from jax import lax
from jax.experimental import pallas as pl
from jax.experimental.pallas import tpu as pltpu

F32 = jnp.float32
BF16 = jnp.bfloat16

EPS = 1e-6
CONV_WIDTH = 3
N_MOD = 6
HEAD_DIM = 128
HGRN_CHUNK = 64
LANES = 128
SUBLANES = 8
VMEM_LIMIT = 56 * 1024 * 1024
LOG2E = 1.4426950408889634


def _params(*sem):
    return pltpu.CompilerParams(dimension_semantics=sem, vmem_limit_bytes=VMEM_LIMIT)


def _tile(n, pref):
    if n <= pref:
        return n
    t = pref
    while n % t:
        t //= 2
    return t


def _ada_kernel(cb_ref, w_ref, b_ref, o_ref, cond_ref):
    @pl.when(pl.program_id(0) == 0)
    def _():
        cb = cb_ref[...]
        cond_ref[...] = cb * jax.nn.sigmoid(cb)

    tn = o_ref.shape[1]
    for j in range(tn // LANES):
        sl = slice(j * LANES, (j + 1) * LANES)
        col = jnp.sum(w_ref[:, sl] * cond_ref[...], axis=0, keepdims=True)
        o_ref[:, sl] = col + b_ref[:, sl]


def _ada_mod(c, w_ada, b_ada):
    d, n = w_ada.shape
    cb = jnp.broadcast_to(c.reshape(d, 1), (d, LANES))
    tn = _tile(n, 512)
    return pl.pallas_call(
        _ada_kernel,
        out_shape=jax.ShapeDtypeStruct((1, n), F32),
        grid=(n // tn,),
        in_specs=[pl.BlockSpec((d, LANES), lambda j: (0, 0)),
                  pl.BlockSpec((d, tn), lambda j: (0, j)),
                  pl.BlockSpec((1, tn), lambda j: (0, j))],
        out_specs=pl.BlockSpec((1, tn), lambda j: (0, j)),
        scratch_shapes=[pltpu.VMEM((d, LANES), F32)],
        compiler_params=_params("arbitrary"),
        name="ada_mod",
    )(cb, w_ada, b_ada.reshape(1, n))


def _norm_mod_kernel(x_ref, g_ref, sc_ref, sh_ref, o_ref):
    x = x_ref[...]
    y = x * lax.rsqrt(jnp.mean(x * x, axis=-1, keepdims=True) + EPS) * g_ref[...]
    o_ref[...] = (y * (1.0 + sc_ref[...]) + sh_ref[...]).astype(o_ref.dtype)


def _norm_kernel(x_ref, g_ref, o_ref):
    x = x_ref[...]
    y = x * lax.rsqrt(jnp.mean(x * x, axis=-1, keepdims=True) + EPS) * g_ref[...]
    o_ref[...] = y.astype(o_ref.dtype)


def _norm_mod(x, g, sc, sh, out_dtype):
    t, d = x.shape
    tm = _tile(t, 256)
    row = pl.BlockSpec((tm, d), lambda i: (i, 0))
    vec = pl.BlockSpec((1, d), lambda i: (0, 0))
    if sc is None:
        kern, args, specs = _norm_kernel, (x, g), [row, vec]
    else:
        kern, args, specs = _norm_mod_kernel, (x, g, sc, sh), [row, vec, vec, vec]
    return pl.pallas_call(
        kern,
        out_shape=jax.ShapeDtypeStruct((t, d), out_dtype),
        grid=(t // tm,),
        in_specs=specs,
        out_specs=row,
        compiler_params=_params("parallel"),
        name="rms_norm",
    )(*args)


def _proj_kernel(*refs, n_a, has_res, lead_tiles, lead_scale, transposed, out_transposed):
    a_refs = refs[:n_a]
    w_ref = refs[n_a]
    pos = n_a + 1
    if has_res:
        x_ref, g_ref = refs[pos], refs[pos + 1]
        pos += 2
    o_ref, wb_ref = refs[pos], refs[pos + 1]

    @pl.when(pl.program_id(1) == 0)
    def _():
        wb_ref[...] = w_ref[...].astype(BF16)

    if out_transposed:
        acc = lax.dot_general(wb_ref[...], a_refs[0][...], (((1,), (1,)), ((), ())),
                              preferred_element_type=F32)
        if lead_tiles:
            acc = acc * jnp.where(pl.program_id(0) < lead_tiles, lead_scale, 1.0).astype(F32)
        o_ref[...] = acc.astype(o_ref.dtype)
        return

    acc = None
    k0 = 0
    for a_ref in a_refs:
        ks = a_ref.shape[1]
        if transposed:
            part = lax.dot_general(a_ref[...], wb_ref[:, k0:k0 + ks], (((1,), (1,)), ((), ())),
                                   preferred_element_type=F32)
        else:
            part = jnp.dot(a_ref[...], wb_ref[k0:k0 + ks, :], preferred_element_type=F32)
        acc = part if acc is None else acc + part
        k0 += ks
    if lead_tiles:
        acc = acc * jnp.where(pl.program_id(0) < lead_tiles, lead_scale, 1.0).astype(F32)
    if has_res:
        acc = x_ref[...] + g_ref[...] * acc
    o_ref[...] = acc.astype(o_ref.dtype)


def _proj(a_list, w, col0, n, out_dtype, res=None, gate=None, lead_cols=0, lead_scale=1.0,
          transposed=False, out_transposed=False, name="proj"):
    m = a_list[0].shape[0]
    k = w.shape[1] if transposed else w.shape[0]
    assert sum(a.shape[1] for a in a_list) == k
    tm = _tile(m, 1024)
    tn = _tile(n, 512)
    assert lead_cols % tn == 0
    assert not out_transposed or (transposed and len(a_list) == 1 and res is None)
    in_specs = [pl.BlockSpec((tm, a.shape[1]), lambda j, i: (i, 0)) for a in a_list]
    if transposed:
        assert col0 % SUBLANES == 0
        in_specs.append(pl.BlockSpec((pl.Element(tn), pl.Element(k)),
                                     lambda j, i: (pl.multiple_of(col0 + j * tn, SUBLANES), 0)))
        wb_shape = (tn, k)
    else:
        assert col0 % tn == 0
        in_specs.append(pl.BlockSpec((k, tn), lambda j, i: (0, col0 // tn + j)))
        wb_shape = (k, tn)
    args = list(a_list) + [w]
    if res is not None:
        in_specs += [pl.BlockSpec((tm, tn), lambda j, i: (i, j)),
                     pl.BlockSpec((1, tn), lambda j, i: (0, j))]
        args += [res, gate]
    return pl.pallas_call(
        functools.partial(_proj_kernel, n_a=len(a_list), has_res=res is not None,
                          lead_tiles=lead_cols // tn, lead_scale=lead_scale, transposed=transposed,
                          out_transposed=out_transposed),
        out_shape=jax.ShapeDtypeStruct((n, m) if out_transposed else (m, n), out_dtype),
        grid=(n // tn, m // tm),
        in_specs=in_specs,
        out_specs=(pl.BlockSpec((tn, tm), lambda j, i: (j, i)) if out_transposed
                   else pl.BlockSpec((tm, tn), lambda j, i: (i, j))),
        scratch_shapes=[pltpu.VMEM(wb_shape, BF16)],
        compiler_params=_params("parallel", "arbitrary"),
        name=name,
    )(*args)


def _split3(x):
    hi = x.astype(BF16)
    r1 = x - hi.astype(F32)
    mid = r1.astype(BF16)
    lo = (r1 - mid.astype(F32)).astype(BF16)
    return hi, mid, lo


QA_ROWS = 8


def _bias_scatter(heads):
    w = heads * LANES
    scat_k = np.zeros((3 * LANES, w), np.float32)
    const_k = np.zeros((1, w), np.float32)
    scat_q = np.zeros((heads * QA_ROWS, 3 * LANES), np.float32)
    const_q = np.zeros((heads * QA_ROWS, 1), np.float32)
    for h in range(heads):
        for piece in range(3):
            scat_k[piece * LANES + h, h * LANES + 3 + piece] = -1.0
            const_k[0, h * LANES + piece] = 1.0
            scat_q[h * QA_ROWS + piece, piece * LANES + h] = 1.0
            const_q[h * QA_ROWS + 3 + piece, 0] = 1.0
    return scat_k, const_k, scat_q, const_q


def _emit_bias_terms(ff, b_ref, scat_ref, const_ref, scatq_ref, constq_ref, qat_ref, ka_ref, carry_ref):
    z = ff + b_ref[...]
    logf = jnp.minimum(z, 0.0) - jnp.log1p(jnp.exp(-jnp.abs(z)))
    tt = z.shape[0]
    row = lax.broadcasted_iota(jnp.int32, (tt, tt), 0)
    col = lax.broadcasted_iota(jnp.int32, (tt, tt), 1)
    tril = jnp.where(col <= row, 1.0, 0.0).astype(BF16)
    local = None
    for piece in _split3(logf):
        part = jnp.dot(tril, piece, preferred_element_type=F32)
        local = part if local is None else local + part
    cum = local + carry_ref[0:1, :]
    carry_ref[...] = jnp.broadcast_to(cum[tt - 1:tt, :], carry_ref.shape)

    pieces = jnp.concatenate(_split3(cum * LOG2E), axis=1)
    ka_ref[...] = (jnp.dot(pieces, scat_ref[...], preferred_element_type=F32) + const_ref[...]).astype(BF16)
    qat_ref[...] = lax.dot_general(scatq_ref[...], pieces, (((1,), (1,)), ((), ())),
                                   preferred_element_type=F32) + constq_ref[...]


def _norm_fox_kernel(x_ref, g_ref, sc_ref, sh_ref, wf_ref, b_ref, scat_ref, const_ref, scatq_ref, constq_ref,
                     o_ref, qat_ref, ka_ref, wfb_ref, carry_ref):
    @pl.when(pl.program_id(0) == 0)
    def _():
        wfb_ref[...] = wf_ref[...].astype(BF16)
        carry_ref[...] = jnp.zeros_like(carry_ref)

    x = x_ref[...]
    y = x * lax.rsqrt(jnp.mean(x * x, axis=-1, keepdims=True) + EPS) * g_ref[...]
    h = (y * (1.0 + sc_ref[...]) + sh_ref[...]).astype(o_ref.dtype)
    o_ref[...] = h
    ff = lax.dot_general(h, wfb_ref[...], (((1,), (1,)), ((), ())), preferred_element_type=F32)
    _emit_bias_terms(ff, b_ref, scat_ref, const_ref, scatq_ref, constq_ref, qat_ref, ka_ref, carry_ref)


def _norm_mod_fox(x, g, sc, sh, w_t, f_row0, b_pad, heads):
    t, d = x.shape
    tm = _tile(t, 512)
    row = pl.BlockSpec((tm, d), lambda i: (i, 0))
    vec = pl.BlockSpec((1, d), lambda i: (0, 0))
    full = lambda a: pl.BlockSpec(a.shape, lambda i: (0, 0))
    scat_k, const_k, scat_q, const_q = _bias_scatter(heads)
    return pl.pallas_call(
        _norm_fox_kernel,
        out_shape=(jax.ShapeDtypeStruct((t, d), BF16),
                   jax.ShapeDtypeStruct((heads * QA_ROWS, t), F32),
                   jax.ShapeDtypeStruct((t, heads * LANES), BF16)),
        grid=(t // tm,),
        in_specs=[row, vec, vec, vec,
                  pl.BlockSpec((pl.Element(LANES), pl.Element(d)), lambda i: (f_row0, 0)),
                  pl.BlockSpec((1, LANES), lambda i: (0, 0)),
                  full(scat_k), full(const_k), full(scat_q), full(const_q)],
        out_specs=(row, pl.BlockSpec((heads * QA_ROWS, tm), lambda i: (0, i)),
                   pl.BlockSpec((tm, heads * LANES), lambda i: (i, 0))),
        scratch_shapes=[pltpu.VMEM((LANES, d), BF16), pltpu.VMEM((SUBLANES, LANES), F32)],
        compiler_params=_params("arbitrary"),
        name="rms_norm_fox_bias",
    )(x, g, sc, sh, w_t, b_pad, jnp.asarray(scat_k, BF16), jnp.asarray(const_k, F32),
      jnp.asarray(scat_q, BF16), jnp.asarray(const_q, F32))


def _zero_from(x):
    bits = pltpu.bitcast(x, jnp.uint32)
    acc = None
    for r in range(0, bits.shape[0], SUBLANES):
        for c in range(0, bits.shape[1], LANES):
            blk = bits[r:r + SUBLANES, c:c + LANES]
            acc = blk if acc is None else acc | blk
    return (acc >> 16) >> 16


def _tie(x, zero):
    pack = 2 * SUBLANES
    z = jnp.tile(pltpu.bitcast(zero, BF16), (1, x.shape[1] // LANES))
    return jnp.concatenate([x[0:pack, :] + z, x[pack:, :]], axis=0)


V_ROWS = HEAD_DIM + 16


FOX_KEY_GROUPS = 8


def _fox_kernel(qt_ref, qat_ref, k_ref, ka_ref, vt_ref, g_ref, o_ref,
                qc_ref, s_ref, mt_ref, m_ref, acc_ref):
    i = pl.program_id(1)
    tq = qt_ref.shape[1]
    tk = tq
    hp = vt_ref.shape[0] // HEAD_DIM
    cols = lambda hh: slice(hh * HEAD_DIM, (hh + 1) * HEAD_DIM)
    ones = jnp.ones((V_ROWS - HEAD_DIM, tk), BF16)
    groups = FOX_KEY_GROUPS if tk % (FOX_KEY_GROUPS * 2 * SUBLANES) == 0 else 1
    gk = tk // groups

    for hh in range(hp):
        qc_ref[hh, 0:HEAD_DIM, :] = qt_ref[cols(hh), :]
        bias_rows = jnp.concatenate([qat_ref[hh * QA_ROWS:(hh + 1) * QA_ROWS, :],
                                     jnp.zeros((HEAD_DIM - QA_ROWS, tq), F32)], axis=0)
        qc_ref[hh, HEAD_DIM:2 * HEAD_DIM, :] = bias_rows.astype(BF16)
    m_ref[...] = jnp.full_like(m_ref, -jnp.inf)
    acc_ref[...] = jnp.zeros_like(acc_ref)

    def scores(hh, j, masked, zeros=None):
        k0 = pl.multiple_of(j * tk, tk)
        kc = jnp.concatenate([k_ref[pl.ds(k0, tk), cols(hh)], ka_ref[pl.ds(k0, tk), cols(hh)]], axis=1)
        if zeros is not None:
            kc = jnp.concatenate([_tie(kc[g * gk:(g + 1) * gk], zeros[g]) for g in range(groups)], axis=0)
        s = jnp.dot(kc, qc_ref[hh], preferred_element_type=F32)
        if masked:
            key = lax.broadcasted_iota(jnp.int32, (tk, tq), 0)
            qry = lax.broadcasted_iota(jnp.int32, (tk, tq), 1)
            s = jnp.where(key <= qry, s, -jnp.inf)
        s_ref[hh] = s
        mt_ref[hh] = jnp.max(s, axis=0, keepdims=True)

    def step(j, next_masked):
        k0 = pl.multiple_of(j * tk, tk)
        for hh in range(hp):
            m_old = m_ref[hh]
            m_new = jnp.maximum(m_old, mt_ref[hh])
            alpha = jnp.exp2(m_old - m_new)
            ps = [jnp.exp2(s_ref[hh, g * gk:(g + 1) * gk, :] - m_new).astype(BF16) for g in range(groups)]
            if next_masked is not None:
                scores(hh, j + 1, next_masked, [_zero_from(pg) for pg in ps])
            vt = jnp.concatenate([vt_ref[cols(hh), pl.ds(k0, tk)], ones], axis=0)
            pv = jnp.dot(vt, jnp.concatenate(ps, axis=0), preferred_element_type=F32)
            acc_ref[hh] = alpha * acc_ref[hh] + pv
            m_ref[hh] = m_new

    @pl.when(i == 0)
    def _():
        for hh in range(hp):
            scores(hh, 0, True)

    @pl.when(i > 0)
    def _():
        for hh in range(hp):
            scores(hh, 0, False)

        def body(jj, carry):
            step(2 * jj, False)
            step(2 * jj + 1, False)
            return carry

        lax.fori_loop(0, (i - 1) // 2, body, 0)

        @pl.when((i - 1) % 2 == 1)
        def _():
            step(i - 2, False)

        step(i - 1, True)

    step(i, None)

    for hh in range(hp):
        acc = acc_ref[hh]
        o = (acc[0:HEAD_DIM, :] / acc[HEAD_DIM:HEAD_DIM + 1, :]).T
        y = o * lax.rsqrt(jnp.mean(o * o, axis=-1, keepdims=True) + EPS) * g_ref[:, cols(hh)]
        o_ref[:, cols(hh)] = y.astype(o_ref.dtype)


def _fox_attention(qt, qat, k, ka, vt, g_fox, heads):
    t = k.shape[0]
    tq = _tile(t, 512)
    hp = 2 if heads % 2 == 0 else 1
    w = hp * HEAD_DIM
    per_group = pl.BlockSpec((t, w), lambda h, i: (0, h))
    return pl.pallas_call(
        _fox_kernel,
        out_shape=jax.ShapeDtypeStruct((t, heads * HEAD_DIM), BF16),
        grid=(heads // hp, t // tq),
        in_specs=[pl.BlockSpec((w, tq), lambda h, i: (h, i)),
                  pl.BlockSpec((hp * QA_ROWS, tq), lambda h, i: (h, i)),
                  per_group, per_group,
                  pl.BlockSpec((w, t), lambda h, i: (h, 0)),
                  pl.BlockSpec((1, w), lambda h, i: (0, h))],
        out_specs=pl.BlockSpec((tq, w), lambda h, i: (i, h)),
        scratch_shapes=[pltpu.VMEM((hp, 2 * HEAD_DIM, tq), BF16),
                        pltpu.VMEM((hp, tq, tq), F32), pltpu.VMEM((hp, 1, tq), F32),
                        pltpu.VMEM((hp, 1, tq), F32), pltpu.VMEM((hp, V_ROWS, tq), F32)],
        compiler_params=_params("parallel", "arbitrary"),
        name="fox_attention",
    )(qt, qat, k, ka, vt, g_fox)


def _hgrn_sum_matrix(c):
    levels = int(math.log2(c))
    p = np.arange(c)[:, None]
    j = np.arange(c)[None, :]
    blocks = []
    for lv in range(levels):
        h = c >> (lv + 1)
        second = ((p // h) % 2) == 1
        m_second = (p // h) * h
        m_first = (p // h + 1) * h
        blocks.append(np.where(second, (j >= m_second) & (j <= p), (j > p) & (j < m_first)))
    blocks.append(j <= p)
    blocks.append(j > p)
    return np.concatenate(blocks, axis=0).astype(np.float32)


def _hgrn_kernel(q_ref, f_ref, i_ref, gate_ref, lbl_ref, g_ref, nmat_ref, o_ref, st_ref, *, chunk, layer):
    tt = q_ref.shape[0]
    levels = int(math.log2(chunk))
    pair = 2 * chunk
    nt = lambda a, b: lax.dot_general(a, b, (((1,), (1,)), ((), ())), preferred_element_type=F32)

    @pl.when(pl.program_id(1) == 0)
    def _():
        st_ref[...] = jnp.zeros_like(st_ref)

    lbl = lbl_ref[...]
    e = jnp.exp(lbl - jnp.max(lbl, axis=0, keepdims=True))
    lb = jnp.sum(e[0:layer + 1, :], axis=0, keepdims=True) / jnp.sum(e, axis=0, keepdims=True)

    rowp = lax.broadcasted_iota(jnp.int32, (pair, HEAD_DIM), 0)
    tpos = lax.broadcasted_iota(jnp.int32, (pair, pair), 0)
    spos = lax.broadcasted_iota(jnp.int32, (pair, pair), 1)
    xor = jnp.bitwise_xor(tpos, spos)
    level_of = jnp.where((spos < tpos) & (xor < chunk), 0, -1)
    for lv in range(1, levels):
        level_of = jnp.where((spos < tpos) & (xor < (chunk >> lv)), lv, level_of)
    nmat = nmat_ref[...]

    npairs = tt // pair
    qf, kk, dec, a = {}, {}, {}, {}
    st = st_ref[...]

    def block(p, blk):
        d = dec[p][blk * chunk:(blk + 1) * chunk]
        return jnp.concatenate([d[:, :HEAD_DIM], d[:, HEAD_DIM:]], axis=0)

    def decay_stage(p):
        r = slice(p * pair, (p + 1) * pair)
        fg = lb + (1.0 - lb) * jax.nn.sigmoid(f_ref[r, :])
        logf = jnp.log(fg)
        kk[p] = 1.0 - fg
        qx = q_ref[r, :]
        qf[p] = qx * jax.nn.sigmoid(qx)
        hi = logf.astype(BF16)
        lo = (logf - hi.astype(F32)).astype(BF16)
        rhs = jnp.concatenate([jnp.concatenate([hi[:chunk], lo[:chunk]], axis=0),
                               jnp.concatenate([hi[chunk:], lo[chunk:]], axis=0)], axis=1)
        dec[p] = jnp.exp(jnp.dot(nmat, rhs, preferred_element_type=F32))

    def intra_stage(p):
        acc = jnp.where(xor == 0, nt(qf[p].astype(BF16), kk[p].astype(BF16)), 0.0)
        for lv in range(levels):
            second = jnp.bitwise_and(rowp, chunk >> (lv + 1)) != 0
            xl = (jnp.where(second, qf[p], kk[p]) * block(p, lv)).astype(BF16)
            acc = jnp.where(level_of == lv, nt(xl, xl), acc)
        a[p] = acc.astype(BF16)

    def state_stage(p, st):
        r = slice(p * pair, (p + 1) * pair)
        d_inc, d_suf = block(p, levels), block(p, levels + 1)
        qd = (qf[p] * d_inc).astype(BF16)
        kd = (kk[p] * d_suf).astype(BF16)
        ivp = i_ref[r, :]
        o_intra = jnp.dot(a[p], ivp, preferred_element_type=F32)
        o_inter = []
        for c in range(0, pair, chunk):
            upd = lax.dot_general(ivp[c:c + chunk], kd[c:c + chunk], (((0,), (0,)), ((), ())),
                                  preferred_element_type=F32)
            o_inter.append(nt(qd[c:c + chunk], st.astype(BF16)))
            st = d_inc[c + chunk - 1:c + chunk, :] * st + upd
        o = o_intra + jnp.concatenate(o_inter, axis=0)
        y = o * lax.rsqrt(jnp.mean(o * o, axis=-1, keepdims=True) + EPS) * g_ref[...]
        gx = gate_ref[r, :]
        o_ref[r, :] = (y * (gx * jax.nn.sigmoid(gx))).astype(o_ref.dtype)
        return st

    for t in range(npairs + 2):
        if t < npairs:
            decay_stage(t)
        if 1 <= t <= npairs:
            intra_stage(t - 1)
        if t >= 2:
            st = state_stage(t - 2, st)
    st_ref[...] = st


def _hgrn2(qf_proj, i_proj, g_proj, lb_logits, g_out, heads, layer):
    t = i_proj.shape[0]
    chunk = min(HGRN_CHUNK, t)
    tt = _tile(t, 64 * chunk)
    sums = _hgrn_sum_matrix(chunk)
    nmat = jnp.asarray(np.concatenate([sums, sums], axis=1), dtype=BF16)
    nl = lb_logits.shape[0]
    blk = lambda off: pl.BlockSpec((tt, HEAD_DIM), lambda h, i: (i, off + h))
    return pl.pallas_call(
        functools.partial(_hgrn_kernel, chunk=chunk, layer=layer),
        out_shape=jax.ShapeDtypeStruct((t, heads * HEAD_DIM), BF16),
        grid=(heads, t // tt),
        in_specs=[blk(0), blk(heads), blk(0), blk(0),
                  pl.BlockSpec((nl, HEAD_DIM), lambda h, i: (0, h)),
                  pl.BlockSpec((1, HEAD_DIM), lambda h, i: (0, h)),
                  pl.BlockSpec(nmat.shape, lambda h, i: (0, 0))],
        out_specs=blk(0),
        scratch_shapes=[pltpu.VMEM((HEAD_DIM, HEAD_DIM), F32)],
        compiler_params=_params("parallel", "arbitrary"),
        name="hgrn2",
    )(qf_proj, qf_proj, i_proj, g_proj, lb_logits, g_out, nmat)


MXU_DEPTH = 256


def _ffn_up_kernel(h_ref, wa_ref, wv_ref, cwa_ref, cwv_ref, cba_ref, cbv_ref, wd_ref, o_ref, wdb_ref,
                   wab_ref, wvb_ref, ua_ref, uv_ref, *, n_row_tiles, n_steps):
    tm, d = h_ref.shape
    wdb_ref[...] = wd_ref[...].astype(BF16)
    halo = SUBLANES
    slabs = d // MXU_DEPTH if d % MXU_DEPTH == 0 and tm % (d // MXU_DEPTH * SUBLANES) == 0 else 1
    sub = tm // slabs
    s = pl.program_id(0)
    row_tile = jnp.minimum(s, n_steps - 1) % n_row_tiles
    cur = s % 2
    prev = 1 - cur

    @pl.when(s == 0)
    def _():
        ua_ref[1] = jnp.zeros(ua_ref.shape[1:], F32)
        uv_ref[1] = jnp.zeros(uv_ref.shape[1:], F32)

    @pl.when(jnp.logical_and(row_tile == 0, s < n_steps))
    def _():
        wab_ref[...] = wa_ref[...].astype(BF16)
        wvb_ref[...] = wv_ref[...].astype(BF16)

    zeros = []
    for r in range(slabs):
        base = halo + r * sub

        def conv(cw_ref, cb_ref, u_ref):
            y = cb_ref[...] + cw_ref[0:1, :] * u_ref[prev, base - 2:base - 2 + sub, :]
            y = y + cw_ref[1:2, :] * u_ref[prev, base - 1:base - 1 + sub, :]
            return y + cw_ref[2:3, :] * u_ref[prev, base:base + sub, :]

        ya = conv(cwa_ref, cba_ref, ua_ref)
        yv = conv(cwv_ref, cbv_ref, uv_ref)
        act = ya * jax.nn.sigmoid(ya) * yv
        o_ref[r * sub:(r + 1) * sub, :] = act.astype(o_ref.dtype)
        zeros.append(_zero_from(act))

    first = row_tile == 0
    kw = d // slabs
    lhs = jnp.concatenate([_tie(h_ref[:, k * kw:(k + 1) * kw], zeros[k]) for k in range(slabs)], axis=1)
    for w_ref, u_ref in ((wab_ref, ua_ref), (wvb_ref, uv_ref)):
        u_ref[cur, 0:halo, :] = jnp.where(first, 0.0, u_ref[prev, tm:tm + halo, :])
        u_ref[cur, halo:halo + tm, :] = jnp.dot(lhs, w_ref[...], preferred_element_type=F32)


def _ffn_up(h, w_up, conv_w, conv_b, w_down):
    t, d = h.shape
    dff = w_up.shape[1] // 2
    tm = _tile(t, 1024)
    tn = _tile(dff, 256)
    nj = dff // tn
    ni = t // tm
    n_steps = nj * ni
    cur = lambda s: jnp.minimum(s, n_steps - 1)
    lag = lambda s: jnp.maximum(s - 1, 0)
    wd_rows = dff // n_steps
    assert dff % n_steps == 0 and wd_rows % (2 * SUBLANES) == 0
    wd_spec = pl.BlockSpec((wd_rows, w_down.shape[1]), lambda s: (cur(s), 0))
    return pl.pallas_call(
        functools.partial(_ffn_up_kernel, n_row_tiles=ni, n_steps=n_steps),
        out_shape=(jax.ShapeDtypeStruct((t, dff), BF16), jax.ShapeDtypeStruct(w_down.shape, BF16)),
        grid=(n_steps + 1,),
        in_specs=[pl.BlockSpec((tm, d), lambda s: (cur(s) % ni, 0)),
                  pl.BlockSpec((d, tn), lambda s: (0, cur(s) // ni)),
                  pl.BlockSpec((d, tn), lambda s: (0, nj + cur(s) // ni)),
                  pl.BlockSpec((CONV_WIDTH, tn), lambda s: (0, lag(s) // ni)),
                  pl.BlockSpec((CONV_WIDTH, tn), lambda s: (0, nj + lag(s) // ni)),
                  pl.BlockSpec((1, tn), lambda s: (0, lag(s) // ni)),
                  pl.BlockSpec((1, tn), lambda s: (0, nj + lag(s) // ni)),
                  wd_spec],
        out_specs=(pl.BlockSpec((tm, tn), lambda s: (lag(s) % ni, lag(s) // ni)), wd_spec),
        scratch_shapes=[pltpu.VMEM((d, tn), BF16), pltpu.VMEM((d, tn), BF16),
                        pltpu.VMEM((2, tm + SUBLANES, tn), F32), pltpu.VMEM((2, tm + SUBLANES, tn), F32)],
        compiler_params=_params("arbitrary"),
        name="ffn_up_conv_glu",
    )(h, w_up, w_up, conv_w, conv_w, conv_b, conv_b, w_down)


def _ffn_down_kernel(a_ref, w_ref, x_ref, g_ref, o_ref):
    acc = jnp.dot(a_ref[...], w_ref[...], preferred_element_type=F32)
    o_ref[...] = x_ref[...] + g_ref[...] * acc


def _ffn_down(act, w_down_bf16, x, gate):
    t, k = act.shape
    n = w_down_bf16.shape[1]
    tm = _tile(t, 512)
    tn = _tile(n, 512)
    return pl.pallas_call(
        _ffn_down_kernel,
        out_shape=jax.ShapeDtypeStruct((t, n), F32),
        grid=(t // tm, n // tn),
        in_specs=[pl.BlockSpec((tm, k), lambda i, j: (i, 0)),
                  pl.BlockSpec((k, tn), lambda i, j: (0, j)),
                  pl.BlockSpec((tm, tn), lambda i, j: (i, j)),
                  pl.BlockSpec((1, tn), lambda i, j: (0, j))],
        out_specs=pl.BlockSpec((tm, tn), lambda i, j: (i, j)),
        compiler_params=_params("parallel", "parallel"),
        name="ffn_down",
    )(act, w_down_bf16, x, gate)


def kernel(x, c, w_ada, b_ada, g_mix_norm, w_in, b_fox_f, hgrn_lb_logits, g_fox_out, g_hgrn_out,
           w_out, g_ffn_norm, w_up, conv_w, conv_b, w_down, g_final):
    b, t, d = x.shape
    assert b == 1, "single-sequence layer"
    depth = w_ada.shape[0]
    fox_heads = b_fox_f.shape[1]
    fox_width = g_fox_out.shape[1]
    hg_kwidth = hgrn_lb_logits.shape[1]
    hg_vwidth = g_hgrn_out.shape[1]
    hg_heads = hg_vwidth // HEAD_DIM
    assert fox_width == fox_heads * HEAD_DIM and hg_kwidth == hg_heads * HEAD_DIM
    assert fox_heads <= LANES
    fox_f0 = 3 * fox_width
    hg_q0 = fox_f0 + fox_heads

    xs = x.reshape(t, d)
    row = lambda v: v.reshape(1, -1)
    for l in range(depth):
        mod = _ada_mod(c, w_ada[l], b_ada[l])
        sh1, sc1, gt1, sh2, sc2, gt2 = [mod[:, k * d:(k + 1) * d] for k in range(N_MOD)]

        w_t = jnp.swapaxes(w_in[l], 0, 1)
        b_ff = jnp.pad(b_fox_f[l], (0, LANES - fox_heads)).reshape(1, LANES)
        h1, qat, ka = _norm_mod_fox(xs, row(g_mix_norm[l]), sc1, sh1, w_t, fox_f0, b_ff, fox_heads)
        in_proj = functools.partial(_proj, [h1], w_t, transposed=True)

        qt = in_proj(0, fox_width, BF16, out_transposed=True, lead_cols=fox_width,
                     lead_scale=LOG2E * HEAD_DIM ** -0.5, name="proj_fox_qt")
        kx = in_proj(fox_width, fox_width, BF16, name="proj_fox_k")
        vt = in_proj(2 * fox_width, fox_width, BF16, out_transposed=True, name="proj_fox_vt")
        hqf = in_proj(hg_q0, 2 * hg_kwidth, F32, name="proj_hgrn_qf")
        hi = in_proj(hg_q0 + 2 * hg_kwidth, hg_vwidth, BF16, name="proj_hgrn_i")
        hg = in_proj(hg_q0 + 2 * hg_kwidth + hg_vwidth, hg_vwidth, F32, name="proj_hgrn_g")

        o_fox = _fox_attention(qt, qat, kx, ka, vt, row(g_fox_out[l]), fox_heads)
        o_hg = _hgrn2(hqf, hi, hg, hgrn_lb_logits, row(g_hgrn_out[l]), hg_heads, l)

        xs = _proj([o_fox, o_hg], w_out[l], 0, d, F32, res=xs, gate=gt1, name="proj_out")

        h2 = _norm_mod(xs, row(g_ffn_norm[l]), sc2, sh2, BF16)
        act, w_down_bf16 = _ffn_up(h2, w_up[l], conv_w[l], row(conv_b[l]), w_down[l])
        xs = _ffn_down(act, w_down_bf16, xs, gt2)

    out = _norm_mod(xs, row(g_final), None, None, x.dtype)
    return out.reshape(b, t, d)
```
